```python
import math
import jax, jax.numpy as jnp
from jax import lax
import numpy as np

D_MODEL = 2048
BATCH = 8
SEQ = 2048
DEPTH = 2
DEC_BATCH = 4
DEC_SEQ = 8192
PAST_LEN = 128

GRID_W = 64
ATT_HEADS = 8
ATT_KV_HEADS = 2
ATT_HEAD_DIM = 128
ATT_GROUPS = ATT_HEADS // ATT_KV_HEADS
ROPE_THETA = 10000.0
AXIS_DIM = ATT_HEAD_DIM // 2
Q_BLOCK = 128
M_HEADS = 4
M_QK_DIM = 128
M_V_DIM = 256
M_CHUNK = 128
M_CONV = 5
N_GATE_KINDS = 4
ATT_Q_W = ATT_HEADS * ATT_HEAD_DIM
ATT_KV_W = ATT_KV_HEADS * ATT_HEAD_DIM
M_QK_W = M_HEADS * M_QK_DIM
M_V_W = M_HEADS * M_V_DIM
M_GATE_W = N_GATE_KINDS * M_HEADS
IN_WIDTH = ATT_Q_W + 2 * ATT_KV_W + 2 * M_QK_W + 2 * M_V_W + M_GATE_W
MIX_WIDTH = ATT_Q_W + M_V_W
N_EXPERTS = 16
EXPERT_FF = 2048
CAPACITY_FACTOR = 2
EPS = 1e-6

kernel_name = "hybrid_mlstm_axial_gqa_ec_encoder"


def rmsnorm(x, g):
    xf = x.astype(jnp.float32)
    y = xf * lax.rsqrt(jnp.mean(xf * xf, axis=-1, keepdims=True) + EPS)
    return (y * g.astype(jnp.float32)).astype(x.dtype)


def axial_rope_tables(T):
    ROWS = T // GRID_W
    rows, cols = jnp.meshgrid(jnp.arange(ROWS), jnp.arange(GRID_W), indexing="ij")
    row = rows.reshape(-1).astype(jnp.float32)
    col = cols.reshape(-1).astype(jnp.float32)
    inv = ROPE_THETA ** (-jnp.arange(0, AXIS_DIM, 2, dtype=jnp.float32) / AXIS_DIM)
    ang_r = row[:, None] * inv[None, :]
    ang_c = col[:, None] * inv[None, :]
    return jnp.cos(ang_r), jnp.sin(ang_r), jnp.cos(ang_c), jnp.sin(ang_c)


def rope_rotate_half(x, cos, sin):
    half = x.shape[-1] // 2
    x1, x2 = x[..., :half], x[..., half:]
    return jnp.concatenate([x1 * cos - x2 * sin, x2 * cos + x1 * sin], axis=-1)


def apply_axial_rope(x, rope):
    cos_r, sin_r, cos_c, sin_c = rope
    return jnp.concatenate([rope_rotate_half(x[..., :AXIS_DIM], cos_r, sin_r),
                            rope_rotate_half(x[..., AXIS_DIM:], cos_c, sin_c)], axis=-1)


def axial_gqa_attention(q, k, v, q_norm_g, k_norm_g, rope):
    B, T, _ = q.shape
    qh = q.reshape(B, T, ATT_KV_HEADS, ATT_GROUPS, ATT_HEAD_DIM)
    kh = k.reshape(B, T, ATT_KV_HEADS, ATT_HEAD_DIM)
    vh = v.reshape(B, T, ATT_KV_HEADS, ATT_HEAD_DIM).transpose(0, 2, 1, 3)
    qh = rmsnorm(qh.astype(jnp.float32), q_norm_g).transpose(0, 2, 3, 1, 4)
    kh = rmsnorm(kh.astype(jnp.float32), k_norm_g).transpose(0, 2, 1, 3)
    qh = apply_axial_rope(qh, rope)
    kh = apply_axial_rope(kh, rope)
    scale = ATT_HEAD_DIM ** -0.5
    NQB = T // Q_BLOCK
    qb = jnp.moveaxis(qh.reshape(B, ATT_KV_HEADS, ATT_GROUPS, NQB, Q_BLOCK, ATT_HEAD_DIM), 3, 0)

    def attend(qblk):
        s = jnp.einsum("bhgqd,bhkd->bhgqk", qblk, kh) * scale
        p = jax.nn.softmax(s, axis=-1)
        return jnp.einsum("bhgqk,bhkd->bhgqd", p.astype(vh.dtype), vh)

    o = lax.map(attend, qb)
    return o.transpose(1, 0, 4, 2, 3, 5).reshape(B, T, ATT_Q_W)


def mlstm_direction(q, k, v, i_pre, f_pre):
    B, H, T, DK = q.shape
    DV = v.shape[-1]
    NC = T // M_CHUNK

    def chunks(a):
        return jnp.moveaxis(a.reshape(B, H, NC, M_CHUNK, *a.shape[3:]), 2, 0)

    k = k * (DK ** -0.5)
    logf = jax.nn.log_sigmoid(f_pre)
    bcum = jnp.cumsum(chunks(logf), axis=-1)
    qs, ks, vs, ig = chunks(q), chunks(k), chunks(v), chunks(i_pre)
    lower = jnp.tril(jnp.ones((M_CHUNK, M_CHUNK), dtype=bool))

    def step(carry, xs):
        C, n, m = carry
        qc, kc, vc, ic, bc = xs
        D = bc[..., :, None] - bc[..., None, :] + ic[..., None, :]
        D = jnp.where(lower, D, -jnp.inf)
        inter = bc + m[..., None]
        m_i = jnp.maximum(inter, jnp.max(D, axis=-1))
        w_intra = jnp.exp(D - m_i[..., None])
        s_inter = jnp.exp(inter - m_i)
        qk = jnp.einsum("bhid,bhjd->bhij", qc, kc) * w_intra
        num = s_inter[..., None] * jnp.einsum("bhid,bhde->bhie", qc, C) + jnp.einsum("bhij,bhje->bhie", qk, vc)
        den = s_inter * jnp.einsum("bhid,bhd->bhi", qc, n) + jnp.sum(qk, axis=-1)
        h = num / jnp.maximum(jnp.abs(den), jnp.exp(-m_i))[..., None]
        bL = bc[..., -1]
        g = bL[..., None] - bc + ic
        m_new = jnp.maximum(bL + m, jnp.max(g, axis=-1))
        s_prev = jnp.exp(bL + m - m_new)
        w_g = jnp.exp(g - m_new[..., None])
        C_new = s_prev[..., None, None] * C + jnp.einsum("bhj,bhjd,bhje->bhde", w_g, kc, vc)
        n_new = s_prev[..., None] * n + jnp.einsum("bhj,bhjd->bhd", w_g, kc)
        return (C_new, n_new, m_new), h

    init = (jnp.zeros((B, H, DK, DV), jnp.float32), jnp.zeros((B, H, DK), jnp.float32),
            jnp.zeros((B, H), jnp.float32))
    _, hs = lax.scan(step, init, (qs, ks, vs, ig, bcum))
    return jnp.moveaxis(hs, 0, 2).reshape(B, H, T, DV)


def bidirectional_mlstm(qk_in, v, o, gates, conv_w, conv_b, mlstm_norm_g):
    B, T, Cqk = qk_in.shape
    qk = lax.conv_general_dilated(qk_in, conv_w.reshape(M_CONV, 1, Cqk), window_strides=(1,),
                                  padding=[(M_CONV // 2, M_CONV // 2)],
                                  dimension_numbers=("NWC", "WIO", "NWC"), feature_group_count=Cqk)
    qk = jax.nn.silu(qk + conv_b)
    qm, km = jnp.split(qk, 2, axis=-1)
    qm = qm.reshape(B, T, M_HEADS, M_QK_DIM).transpose(0, 2, 1, 3).astype(jnp.float32)
    km = km.reshape(B, T, M_HEADS, M_QK_DIM).transpose(0, 2, 1, 3).astype(jnp.float32)
    vm = v.reshape(B, T, M_HEADS, M_V_DIM).transpose(0, 2, 1, 3).astype(jnp.float32)
    g = gates.transpose(2, 0, 3, 1)
    h_fw = mlstm_direction(qm, km, vm, g[0], g[1])
    h_bw = jnp.flip(mlstm_direction(jnp.flip(qm, 2), jnp.flip(km, 2), jnp.flip(vm, 2),
                                    jnp.flip(g[2], 2), jnp.flip(g[3], 2)), 2)
    h = (h_fw + h_bw).transpose(0, 2, 1, 3)
    h = rmsnorm(h, mlstm_norm_g.reshape(M_HEADS, M_V_DIM)).reshape(B, T, M_V_W)
    return (h * jax.nn.sigmoid(o.astype(jnp.float32))).astype(v.dtype)


def hybrid_mixer(h, w_in, conv_w, conv_b, b_gates, q_norm_g, k_norm_g, mlstm_norm_g, w_out, rope):
    B, T, _ = h.shape
    proj = jnp.einsum("btd,de->bte", h, w_in)
    sizes = (ATT_Q_W, ATT_KV_W, ATT_KV_W, 2 * M_QK_W, M_V_W, M_V_W, M_GATE_W)
    a_q, a_k, a_v, m_qk, m_v, m_o, m_g = jnp.split(proj, list(np.cumsum(sizes)[:-1]), axis=-1)
    att = axial_gqa_attention(a_q, a_k, a_v, q_norm_g, k_norm_g, rope)
    gates = m_g.astype(jnp.float32).reshape(B, T, N_GATE_KINDS, M_HEADS) + \
        b_gates.astype(jnp.float32).reshape(N_GATE_KINDS, M_HEADS)
    mem = bidirectional_mlstm(m_qk, m_v, m_o, gates, conv_w, conv_b, mlstm_norm_g)
    mixed = jnp.concatenate([mem, att.astype(mem.dtype)], axis=-1)
    return jnp.einsum("bte,ed->btd", mixed, w_out)


def expert_choice_ffn(h, w_router, w_gate, w_up, w_down):
    B, T, D = h.shape
    N = B * T
    xf = h.reshape(N, D)
    aff = jax.nn.softmax(jnp.einsum("nd,de->ne", xf, w_router).astype(jnp.float32), axis=-1)
    cap = CAPACITY_FACTOR * N // N_EXPERTS
    gate, idx = lax.top_k(aff.T, cap)
    xe = xf[idx]
    hid = jax.nn.silu(jnp.einsum("ecd,edf->ecf", xe, w_gate)) * jnp.einsum("ecd,edf->ecf", xe, w_up)
    ye = jnp.einsum("ecf,efd->ecd", hid, w_down) * gate[..., None].astype(h.dtype)
    out = jnp.zeros((N, D), h.dtype).at[idx.reshape(-1)].add(ye.reshape(-1, D))
    return out.reshape(B, T, D)


def trunk(x, norm1_g, w_in, conv_w, conv_b, b_gates, q_norm_g, k_norm_g, mlstm_norm_g, w_out,
          norm2_g, w_router, w_gate, w_up, w_down, final_norm_g):
    rope = axial_rope_tables(x.shape[1])
    for l in range(DEPTH):
        h = rmsnorm(x, norm1_g[l])
        x = x + hybrid_mixer(h, w_in[l], conv_w[l], conv_b[l], b_gates[l], q_norm_g[l], k_norm_g[l],
                             mlstm_norm_g[l], w_out[l], rope)
        h = rmsnorm(x, norm2_g[l])
        x = x + expert_choice_ffn(h, w_router[l], w_gate[l], w_up[l], w_down[l])
    return rmsnorm(x, final_norm_g)


def setup_inputs(seed: int = 0) -> dict:
    key = jax.random.key(seed)
    ks = jax.random.split(key, 20)
    f32 = jnp.float32
    nrm = lambda k, shape, s: jax.random.normal(k, shape, f32) * s
    gate_offsets = jnp.repeat(jnp.array([0.0, 3.0, 0.0, 3.0], f32), M_HEADS)
    return {
        "x_prompt": nrm(ks[0], (BATCH, SEQ, D_MODEL), 1.0),
        "x_sample": nrm(ks[1], (DEC_BATCH, DEC_SEQ, D_MODEL), 1.0),
        "norm1_g": 1.0 + nrm(ks[2], (DEPTH, D_MODEL), 0.02),
        "w_in": nrm(ks[3], (DEPTH, D_MODEL, IN_WIDTH), D_MODEL ** -0.5),
        "conv_w": nrm(ks[4], (DEPTH, M_CONV, 2 * M_QK_W), M_CONV ** -0.5),
        "conv_b": nrm(ks[5], (DEPTH, 2 * M_QK_W), 0.02),
        "b_gates": gate_offsets + nrm(ks[6], (DEPTH, M_GATE_W), 0.1),
        "q_norm_g": 1.0 + nrm(ks[7], (DEPTH, ATT_HEAD_DIM), 0.02),
        "k_norm_g": 1.0 + nrm(ks[8], (DEPTH, ATT_HEAD_DIM), 0.02),
        "mlstm_norm_g": 1.0 + nrm(ks[9], (DEPTH, M_V_W), 0.02),
        "w_out": nrm(ks[10], (DEPTH, MIX_WIDTH, D_MODEL), MIX_WIDTH ** -0.5),
        "norm2_g": 1.0 + nrm(ks[11], (DEPTH, D_MODEL), 0.02),
        "w_router": nrm(ks[12], (DEPTH, D_MODEL, N_EXPERTS), D_MODEL ** -0.5),
        "w_gate": nrm(ks[13], (DEPTH, N_EXPERTS, D_MODEL, EXPERT_FF), D_MODEL ** -0.5),
        "w_up": nrm(ks[14], (DEPTH, N_EXPERTS, D_MODEL, EXPERT_FF), D_MODEL ** -0.5),
        "w_down": nrm(ks[15], (DEPTH, N_EXPERTS, EXPERT_FF, D_MODEL), EXPERT_FF ** -0.5),
        "final_norm_g": 1.0 + nrm(ks[16], (D_MODEL,), 0.02),
    }


def reference(x_prompt, x_sample, norm1_g, w_in, conv_w, conv_b, b_gates, q_norm_g, k_norm_g,
              mlstm_norm_g, w_out, norm2_g, w_router, w_gate, w_up, w_down, final_norm_g):
    y_prompt = trunk(x_prompt, norm1_g, w_in, conv_w, conv_b, b_gates, q_norm_g, k_norm_g,
                     mlstm_norm_g, w_out, norm2_g, w_router, w_gate, w_up, w_down, final_norm_g)
    y_sample = trunk(x_sample, norm1_g, w_in, conv_w, conv_b, b_gates, q_norm_g, k_norm_g,
                     mlstm_norm_g, w_out, norm2_g, w_router, w_gate, w_up, w_down, final_norm_g)
    return (y_prompt, y_sample)
```

```python
import functools
import math

import jax
import jax.numpy as jnp
from jax import lax
from jax.experimental import pallas as pl
from jax.experimental.pallas import tpu as pltpu

F32 = jnp.float32
BF16 = jnp.bfloat16
EPS = 1e-6
LOG2E = 1.4426950408889634

LANES = 128
SUBLANES = 8
VMEM_BYTES_V7X = 64 * 1024 * 1024
VMEM_LIMIT = VMEM_BYTES_V7X - 8 * 1024 * 1024

GRID_W = 64
ATT_KV_HEADS = 2
ATT_GROUPS = 4
HEAD_DIM = 128
AXIS_DIM = HEAD_DIM // 2
ROPE_THETA = 10000.0
M_HEADS = 4
M_QK_DIM = 128
M_V_DIM = 256
M_CHUNK = 128
M_CONV = 5
N_GATE_KINDS = 4
N_EXPERTS = 16
CAPACITY_FACTOR = 2
ATT_Q_W = ATT_KV_HEADS * ATT_GROUPS * HEAD_DIM
ATT_KV_W = ATT_KV_HEADS * HEAD_DIM
M_QK_W = M_HEADS * M_QK_DIM
M_V_W = M_HEADS * M_V_DIM
M_GATE_W = N_GATE_KINDS * M_HEADS
COL_Q = 0
COL_K = COL_Q + ATT_Q_W
COL_V = COL_K + ATT_KV_W
COL_MQK = COL_V + ATT_KV_W
COL_MV = COL_MQK + 2 * M_QK_W
COL_MO = COL_MV + M_V_W
COL_MG = COL_MO + M_V_W
IN_WIDTH = COL_MG + M_GATE_W
IN_WIDTH_PAD = 4864
M_AUG = M_V_DIM + LANES
MAX_PLANES = N_EXPERTS


def _tile(dim, pref):
    t = min(dim, pref)
    assert dim % t == 0, (dim, pref)
    return t


def _params(sem, vmem=VMEM_LIMIT):
    return pltpu.CompilerParams(dimension_semantics=sem, vmem_limit_bytes=vmem)


def _rms(x, g):
    return x * lax.rsqrt(jnp.mean(x * x, axis=-1, keepdims=True) + EPS) * g


def _log_sigmoid(x):
    return jnp.minimum(x, 0.0) - jnp.log1p(jnp.exp(-jnp.abs(x)))


def _rope(x, cos, sin):
    lane = lax.broadcasted_iota(jnp.int32, x.shape, 1)
    first = (lane % AXIS_DIM) < (AXIS_DIM // 2)
    swapped = jnp.where(first, pltpu.roll(x, LANES - AXIS_DIM // 2, axis=1),
                        pltpu.roll(x, AXIS_DIM // 2, axis=1))
    return x * cos + swapped * sin


def _inproj_kernel(x_ref, g_ref, w_ref, o_ref, *, col_chunk):
    h = _rms(x_ref[...], g_ref[...]).astype(BF16)
    width = o_ref.shape[1]
    for c0 in range(0, width, col_chunk):
        c1 = min(c0 + col_chunk, width)
        o_ref[:, c0:c1] = jnp.dot(h, w_ref[:, c0:c1], preferred_element_type=F32)


def _inproj(x, g, w):
    n, d = x.shape
    p = w.shape[1]
    tm = _tile(n, 256)
    return pl.pallas_call(
        functools.partial(_inproj_kernel, col_chunk=512),
        grid=(n // tm,),
        in_specs=[pl.BlockSpec((tm, d), lambda i: (i, 0)),
                  pl.BlockSpec((1, d), lambda i: (0, 0)),
                  pl.BlockSpec((d, p), lambda i: (0, 0), pipeline_mode=pl.Buffered(1))],
        out_specs=pl.BlockSpec((tm, p), lambda i: (i, 0)),
        out_shape=jax.ShapeDtypeStruct((n, p), F32),
        compiler_params=_params(("parallel",)),
        name="inproj",
    )(x, g, w)


def _kprep_kernel(k_ref, v_ref, cos_ref, sin_ref, g_ref, ko_ref, vt_ref):
    cos, sin, g = cos_ref[...], sin_ref[...], g_ref[...]
    for h in range(ATT_KV_HEADS):
        k = _rms(k_ref[:, h * HEAD_DIM:(h + 1) * HEAD_DIM], g)
        ko_ref[h] = _rope(k, cos, sin).astype(BF16)
        vt_ref[h, 0] = v_ref[:, h * HEAD_DIM:(h + 1) * HEAD_DIM].T.astype(BF16)


def _kprep(proj, cos, sin, gk, tk):
    b, t, _ = proj.shape
    nkb = t // tk
    return pl.pallas_call(
        _kprep_kernel,
        grid=(b, nkb),
        in_specs=[pl.BlockSpec((None, tk, ATT_KV_W), lambda i, j: (i, j, COL_K // ATT_KV_W)),
                  pl.BlockSpec((None, tk, ATT_KV_W), lambda i, j: (i, j, COL_V // ATT_KV_W)),
                  pl.BlockSpec((tk, HEAD_DIM), lambda i, j: (j, 0)),
                  pl.BlockSpec((tk, HEAD_DIM), lambda i, j: (j, 0)),
                  pl.BlockSpec((1, HEAD_DIM), lambda i, j: (0, 0))],
        out_specs=[pl.BlockSpec((None, ATT_KV_HEADS, tk, HEAD_DIM), lambda i, j: (i, 0, j, 0)),
                   pl.BlockSpec((None, ATT_KV_HEADS, 1, HEAD_DIM, tk), lambda i, j: (i, 0, j, 0, 0))],
        out_shape=[jax.ShapeDtypeStruct((b, ATT_KV_HEADS, t, HEAD_DIM), BF16),
                   jax.ShapeDtypeStruct((b, ATT_KV_HEADS, nkb, HEAD_DIM, tk), BF16)],
        compiler_params=_params(("parallel", "parallel")),
        name="kprep",
    )(proj, proj, cos, sin, gk)


def _attn_kernel(q_ref, cos_ref, sin_ref, g_ref, k_ref, vt_ref, o_ref, *, tq, tk, nkb):
    cos, sin, g = cos_ref[...], sin_ref[...], g_ref[...]
    qscale = HEAD_DIM ** -0.5 * LOG2E
    qs = []
    for i in range(ATT_GROUPS):
        q = _rms(q_ref[:, i * HEAD_DIM:(i + 1) * HEAD_DIM], g)
        qs.append((_rope(q, cos, sin) * qscale).astype(BF16))
    q4 = jnp.concatenate(qs, axis=0)
    nq = ATT_GROUPS * tq

    def body(j, carry):
        m, l, acc = carry
        kb = k_ref[pl.ds(pl.multiple_of(j * tk, tk), tk), :]
        s = lax.dot_general(kb, q4, (((1,), (1,)), ((), ())), preferred_element_type=F32)
        m_new = jnp.maximum(m, jnp.max(s, axis=0, keepdims=True))
        alpha = jnp.exp2(m - m_new)
        p = jnp.exp2(s - m_new)
        l = alpha * l + jnp.sum(p, axis=0, keepdims=True)
        acc = alpha * acc + jnp.dot(vt_ref[j], p.astype(BF16), preferred_element_type=F32)
        return m_new, l, acc

    init = (jnp.full((1, nq), -jnp.inf, F32), jnp.zeros((1, nq), F32), jnp.zeros((HEAD_DIM, nq), F32))
    _, l, acc = lax.fori_loop(0, nkb, body, init)
    o = acc / l
    for i in range(ATT_GROUPS):
        o_ref[:, i * HEAD_DIM:(i + 1) * HEAD_DIM] = o[:, i * tq:(i + 1) * tq].T.astype(BF16)


def _attention(proj, kr, vt, cos, sin, gq, tq):
    b, t, _ = proj.shape
    nkb, tk = vt.shape[2], vt.shape[4]
    gw = ATT_GROUPS * HEAD_DIM
    return pl.pallas_call(
        functools.partial(_attn_kernel, tq=tq, tk=tk, nkb=nkb),
        grid=(b, ATT_KV_HEADS, t // tq),
        in_specs=[pl.BlockSpec((None, tq, gw), lambda i, h, j: (i, j, h)),
                  pl.BlockSpec((tq, HEAD_DIM), lambda i, h, j: (j, 0)),
                  pl.BlockSpec((tq, HEAD_DIM), lambda i, h, j: (j, 0)),
                  pl.BlockSpec((1, HEAD_DIM), lambda i, h, j: (0, 0)),
                  pl.BlockSpec((None, None, t, HEAD_DIM), lambda i, h, j: (i, h, 0, 0)),
                  pl.BlockSpec((None, None, nkb, HEAD_DIM, tk), lambda i, h, j: (i, h, 0, 0, 0))],
        out_specs=pl.BlockSpec((None, tq, gw), lambda i, h, j: (i, j, h)),
        out_shape=jax.ShapeDtypeStruct((b, t, ATT_Q_W), BF16),
        compiler_params=_params(("parallel", "parallel", "arbitrary")),
        name="attention",
    )(proj, cos, sin, gq, kr, vt)


def _conv_kernel(x_ref, prev_ref, next_ref, w_ref, b_ref, o_ref, pad_ref, *, tc, n_q_tiles):
    r = pl.program_id(1)
    nr = pl.num_programs(1)
    halo = SUBLANES
    pad_ref[0:halo, :] = jnp.where(r > 0, prev_ref[...], 0.0)
    pad_ref[halo:halo + tc, :] = x_ref[...]
    pad_ref[halo + tc:2 * halo + tc, :] = jnp.where(r < nr - 1, next_ref[...], 0.0)
    acc = jnp.broadcast_to(b_ref[...], x_ref.shape)
    for k in range(M_CONV):
        lo = halo - M_CONV // 2 + k
        acc = acc + w_ref[k:k + 1, :] * pad_ref[lo:lo + tc, :]
    y = acc * jax.nn.sigmoid(acc)
    scale = jnp.where(pl.program_id(2) >= n_q_tiles, M_QK_DIM ** -0.5, 1.0).astype(F32)
    o_ref[...] = (y * scale).astype(BF16)


def _conv(proj, w, bias):
    b, t, _ = proj.shape
    cw = M_QK_W
    tc = _tile(t, 512)
    hb = tc // SUBLANES
    c0 = COL_MQK // cw
    last = t // SUBLANES - 1
    return pl.pallas_call(
        functools.partial(_conv_kernel, tc=tc, n_q_tiles=M_QK_W // cw),
        grid=(b, t // tc, 2 * M_QK_W // cw),
        in_specs=[pl.BlockSpec((None, tc, cw), lambda i, r, c: (i, r, c0 + c)),
                  pl.BlockSpec((None, SUBLANES, cw), lambda i, r, c: (i, jnp.maximum(r * hb - 1, 0), c0 + c)),
                  pl.BlockSpec((None, SUBLANES, cw), lambda i, r, c: (i, jnp.minimum((r + 1) * hb, last), c0 + c)),
                  pl.BlockSpec((M_CONV, cw), lambda i, r, c: (0, c)),
                  pl.BlockSpec((1, cw), lambda i, r, c: (0, c))],
        out_specs=pl.BlockSpec((None, tc, cw), lambda i, r, c: (i, r, c)),
        out_shape=jax.ShapeDtypeStruct((b, t, 2 * M_QK_W), BF16),
        scratch_shapes=[pltpu.VMEM((tc + 2 * SUBLANES, cw), F32)],
        compiler_params=_params(("parallel", "parallel", "parallel")),
        name="conv",
    )(proj, proj, proj, w, bias)


def _mlstm_kernel(*refs, reverse):
    if reverse:
        (q_ref, k_ref, vlo_ref, vhi_ref, gcol_ref, grow_ref, brow_ref, bcol_ref,
         hfw_ref, olo_ref, ohi_ref, gn_ref, out_ref, c_ref, m_ref) = refs
    else:
        (q_ref, k_ref, vlo_ref, vhi_ref, gcol_ref, grow_ref, brow_ref, bcol_ref,
         out_ref, c_ref, m_ref) = refs
    L = M_CHUNK

    @pl.when(pl.program_id(1) == 0)
    def _():
        c_ref[...] = jnp.zeros_like(c_ref)
        m_ref[...] = jnp.zeros_like(m_ref)

    ii = lax.broadcasted_iota(jnp.int32, (L, L), 0)
    jj = lax.broadcasted_iota(jnp.int32, (L, L), 1)
    mask = (jj >= ii) if reverse else (jj <= ii)
    tri = mask.astype(F32)
    tri_t = ((ii >= jj) if reverse else (ii <= jj)).astype(F32)
    kind = 2 * M_HEADS if reverse else 0

    gc = gcol_ref[...] + brow_ref[...]
    logf_c = _log_sigmoid(gc)
    cum_c = jnp.dot(tri, logf_c, preferred_element_type=F32, precision=lax.Precision.HIGHEST)
    tot_c = jnp.sum(logf_c, axis=0, keepdims=True)
    gr = grow_ref[...] + bcol_ref[...]
    logf_r = _log_sigmoid(gr)
    cum_r = jnp.dot(logf_r, tri_t, preferred_element_type=F32, precision=lax.Precision.HIGHEST)

    lane = lax.broadcasted_iota(jnp.int32, (L, LANES), 1)
    ones_col = jnp.where(lane == 0, 1.0, 0.0).astype(BF16)

    for hd in range(M_HEADS):
        li, lf = kind + hd, kind + M_HEADS + hd
        b_col, ig_col = cum_c[:, lf:lf + 1], gc[:, li:li + 1]
        b_row, ig_row = cum_r[lf:lf + 1, :], gr[li:li + 1, :]
        tot = tot_c[:, lf:lf + 1]
        m_prev = m_ref[hd:hd + 1, 0:1]
        q = q_ref[:, hd * M_QK_DIM:(hd + 1) * M_QK_DIM]
        k = k_ref[:, hd * M_QK_DIM:(hd + 1) * M_QK_DIM]
        v_ref = vlo_ref if hd < M_HEADS // 2 else vhi_ref
        vo = (hd % (M_HEADS // 2)) * M_V_DIM
        v_aug = jnp.concatenate([v_ref[:, vo:vo + M_V_DIM].astype(BF16), ones_col], axis=1)

        d = jnp.where(mask, b_col - (b_row - ig_row), -jnp.inf)
        inter = b_col + m_prev
        m_i = jnp.maximum(inter, jnp.max(d, axis=1, keepdims=True))
        w_intra = jnp.exp(d - m_i)
        s_inter = jnp.exp(inter - m_i)
        s = lax.dot_general(q, k, (((1,), (1,)), ((), ())), preferred_element_type=F32) * w_intra
        c_aug = c_ref[hd]
        num = (s_inter * jnp.dot(q, c_aug.astype(BF16), preferred_element_type=F32)
               + jnp.dot(s.astype(BF16), v_aug, preferred_element_type=F32))
        den = num[:, M_V_DIM:M_V_DIM + 1]
        h = num[:, :M_V_DIM] / jnp.maximum(jnp.abs(den), jnp.exp(-m_i))

        gvec = tot - b_col + ig_col
        m_new = jnp.maximum(tot + m_prev, jnp.max(gvec, axis=0, keepdims=True))
        s_prev = jnp.exp(tot + m_prev - m_new)
        kw = (k.astype(F32) * jnp.exp(gvec - m_new)).T.astype(BF16)
        c_ref[hd] = s_prev * c_aug + jnp.dot(kw, v_aug, preferred_element_type=F32)
        m_ref[hd:hd + 1, :] = jnp.broadcast_to(m_new, (1, LANES))

        cs = slice(hd * M_V_DIM, (hd + 1) * M_V_DIM)
        if reverse:
            o_ref = olo_ref if hd < M_HEADS // 2 else ohi_ref
            hs = hfw_ref[:, cs] + h
            y = _rms(hs, gn_ref[:, cs]) * jax.nn.sigmoid(o_ref[:, vo:vo + M_V_DIM])
            out_ref[:, cs] = y.astype(BF16)
        else:
            out_ref[:, cs] = h


def _mlstm(proj, qkm, gates_t, b_row, b_col, reverse, hfw=None, gn=None):
    b, t, _ = proj.shape
    nc = t // M_CHUNK
    half = M_V_W // 2

    def cidx(c):
        return nc - 1 - c if reverse else c

    in_specs = [pl.BlockSpec((None, M_CHUNK, M_QK_W), lambda i, c: (i, cidx(c), 0)),
                pl.BlockSpec((None, M_CHUNK, M_QK_W), lambda i, c: (i, cidx(c), 1)),
                pl.BlockSpec((None, M_CHUNK, half), lambda i, c: (i, cidx(c), COL_MV // half)),
                pl.BlockSpec((None, M_CHUNK, half), lambda i, c: (i, cidx(c), COL_MV // half + 1)),
                pl.BlockSpec((None, M_CHUNK, LANES), lambda i, c: (i, cidx(c), COL_MG // LANES)),
                pl.BlockSpec((None, M_GATE_W, M_CHUNK), lambda i, c: (i, 0, cidx(c))),
                pl.BlockSpec((1, LANES), lambda i, c: (0, 0)),
                pl.BlockSpec((M_GATE_W, 1), lambda i, c: (0, 0))]
    args = [qkm, qkm, proj, proj, proj, gates_t, b_row, b_col]
    if reverse:
        in_specs += [pl.BlockSpec((None, M_CHUNK, M_V_W), lambda i, c: (i, cidx(c), 0)),
                     pl.BlockSpec((None, M_CHUNK, half), lambda i, c: (i, cidx(c), COL_MO // half)),
                     pl.BlockSpec((None, M_CHUNK, half), lambda i, c: (i, cidx(c), COL_MO // half + 1)),
                     pl.BlockSpec((1, M_V_W), lambda i, c: (0, 0))]
        args += [hfw, proj, proj, gn]
    return pl.pallas_call(
        functools.partial(_mlstm_kernel, reverse=reverse),
        grid=(b, nc),
        in_specs=in_specs,
        out_specs=pl.BlockSpec((None, M_CHUNK, M_V_W), lambda i, c: (i, cidx(c), 0)),
        out_shape=jax.ShapeDtypeStruct((b, t, M_V_W), BF16 if reverse else F32),
        scratch_shapes=[pltpu.VMEM((M_HEADS, M_QK_DIM, M_AUG), F32),
                        pltpu.VMEM((SUBLANES, LANES), F32)],
        compiler_params=_params(("parallel", "arbitrary")),
        name="mlstm_bw" if reverse else "mlstm_fw",
    )(*args)


def _outproj_kernel(x_ref, mem_ref, att_ref, w_ref, g_ref, wr_ref, x1_ref, h_ref, aff_ref):
    half = mem_ref.shape[1]
    y = (jnp.dot(mem_ref[...], w_ref[0:half, :], preferred_element_type=F32)
         + jnp.dot(att_ref[...], w_ref[half:2 * half, :], preferred_element_type=F32))
    x1 = x_ref[...] + y
    x1_ref[...] = x1
    h = _rms(x1, g_ref[...])
    h_ref[...] = h
    logits = lax.dot_general(wr_ref[...], h, (((1,), (1,)), ((), ())), preferred_element_type=F32,
                             precision=lax.Precision.HIGHEST)
    e = jnp.exp(logits - jnp.max(logits, axis=0, keepdims=True))
    aff_ref[...] = e / jnp.sum(e, axis=0, keepdims=True)


def _outproj(x, mem, att, w, g, wr_t):
    n, d = x.shape
    half = mem.shape[1]
    tm = _tile(n, 512)
    return pl.pallas_call(
        _outproj_kernel,
        grid=(n // tm,),
        in_specs=[pl.BlockSpec((tm, d), lambda i: (i, 0)),
                  pl.BlockSpec((tm, half), lambda i: (i, 0)),
                  pl.BlockSpec((tm, half), lambda i: (i, 0)),
                  pl.BlockSpec((2 * half, d), lambda i: (0, 0), pipeline_mode=pl.Buffered(1)),
                  pl.BlockSpec((1, d), lambda i: (0, 0)),
                  pl.BlockSpec((N_EXPERTS, d), lambda i: (0, 0))],
        out_specs=[pl.BlockSpec((tm, d), lambda i: (i, 0)),
                   pl.BlockSpec((tm, d), lambda i: (i, 0)),
                   pl.BlockSpec((N_EXPERTS, tm), lambda i: (0, i))],
        out_shape=[jax.ShapeDtypeStruct((n, d), F32),
                   jax.ShapeDtypeStruct((n, d), F32),
                   jax.ShapeDtypeStruct((N_EXPERTS, n), F32)],
        compiler_params=_params(("parallel",)),
        name="outproj_router",
    )(x, mem, att, w, g, wr_t)


def _row_copy(src_hbm, src_row, dst, dst_row, sem):
    return pltpu.make_async_copy(src_hbm.at[pl.ds(src_row, 1), :], dst.at[pl.ds(dst_row, 1), :], sem)


def _ffn_kernel(idx_ref, h_hbm, gate_ref, wg_ref, wu_ref, wd_ref, o_ref, xf_ref, xb_ref, sem, *, tm):
    f = pl.program_id(2)

    @pl.when(f == 0)
    def _():
        def issue(r, carry):
            _row_copy(h_hbm, idx_ref[0, 0, r], xf_ref, r, sem).start()
            return carry
        lax.fori_loop(0, tm, issue, 0)
        pltpu.make_async_copy(h_hbm.at[pl.ds(0, tm), :], xf_ref, sem).wait()
        xb_ref[...] = xf_ref[...].astype(BF16)

    x = xb_ref[...]
    a = jnp.dot(x, wg_ref[...], preferred_element_type=F32)
    u = jnp.dot(x, wu_ref[...], preferred_element_type=F32)
    hid = (a * jax.nn.sigmoid(a) * u).astype(BF16)
    y = jnp.dot(hid, wd_ref[...], preferred_element_type=F32)

    @pl.when(f == 0)
    def _():
        o_ref[...] = y

    @pl.when(f > 0)
    def _():
        o_ref[...] += y

    @pl.when(f == pl.num_programs(2) - 1)
    def _():
        o_ref[...] = o_ref[...] * gate_ref[...]


def _ffn(h, idx, gate, wg, wu, wd):
    n, d = h.shape
    e, cap = idx.shape
    ff = wg.shape[2]
    tm = _tile(cap, 512)
    tf = _tile(ff, 512)
    nt = cap // tm
    idx3 = idx.reshape(e * nt, 1, tm)
    gate2 = gate.reshape(e * cap, 1)
    return pl.pallas_call(
        functools.partial(_ffn_kernel, tm=tm),
        grid=(e, nt, ff // tf),
        in_specs=[pl.BlockSpec((1, 1, tm), lambda i, j, f: (i * nt + j, 0, 0), memory_space=pltpu.SMEM),
                  pl.BlockSpec(memory_space=pl.ANY),
                  pl.BlockSpec((tm, 1), lambda i, j, f: (i * nt + j, 0)),
                  pl.BlockSpec((None, d, tf), lambda i, j, f: (i, 0, f)),
                  pl.BlockSpec((None, d, tf), lambda i, j, f: (i, 0, f)),
                  pl.BlockSpec((None, tf, d), lambda i, j, f: (i, f, 0))],
        out_specs=pl.BlockSpec((tm, d), lambda i, j, f: (i * nt + j, 0)),
        out_shape=jax.ShapeDtypeStruct((e * cap, d), F32),
        scratch_shapes=[pltpu.VMEM((tm, d), F32), pltpu.VMEM((tm, d), BF16), pltpu.SemaphoreType.DMA(())],
        compiler_params=_params(("arbitrary", "arbitrary", "arbitrary")),
        name="expert_ffn",
    )(idx3, h, gate2, wg, wu, wd)


def _combine_kernel(ent_ref, off_ref, jmax_ref, x_ref, cnt_ref, ye_hbm, g_ref, o_ref, buf_ref, sem,
                    *, tb, final_norm):
    t = pl.program_id(0)
    lo = off_ref[t]
    count = off_ref[t + 1] - lo
    jmax = jmax_ref[t]

    for j in range(MAX_PLANES):
        @pl.when(j < jmax)
        def _():
            buf_ref[j * tb:(j + 1) * tb, :] = jnp.zeros((tb, buf_ref.shape[1]), F32)

    def issue(p, carry):
        ent = ent_ref[lo + p]
        _row_copy(ye_hbm, ent & 0xFFFF, buf_ref, ent >> 16, sem).start()
        return carry
    lax.fori_loop(0, count, issue, 0)

    def drain(p, carry):
        _row_copy(ye_hbm, 0, buf_ref, 0, sem).wait()
        return carry
    lax.fori_loop(0, count, drain, 0)

    o_ref[...] = x_ref[...]
    cnt = cnt_ref[...]
    for j in range(MAX_PLANES):
        @pl.when(j < jmax)
        def _():
            o_ref[...] += jnp.where(cnt > j, buf_ref[j * tb:(j + 1) * tb, :], 0.0)

    if final_norm:
        o_ref[...] = _rms(o_ref[...], g_ref[...])


def _combine(x1, ye, entries, offsets, jmax, cnt, g, tb, final_norm):
    n, d = x1.shape
    grid_spec = pltpu.PrefetchScalarGridSpec(
        num_scalar_prefetch=3,
        grid=(n // tb,),
        in_specs=[pl.BlockSpec((tb, d), lambda i, *_: (i, 0)),
                  pl.BlockSpec((tb, 1), lambda i, *_: (i, 0)),
                  pl.BlockSpec(memory_space=pl.ANY),
                  pl.BlockSpec((1, d), lambda i, *_: (0, 0))],
        out_specs=pl.BlockSpec((tb, d), lambda i, *_: (i, 0)),
        scratch_shapes=[pltpu.VMEM((MAX_PLANES * tb, d), F32), pltpu.SemaphoreType.DMA(())],
    )
    return pl.pallas_call(
        functools.partial(_combine_kernel, tb=tb, final_norm=final_norm),
        grid_spec=grid_spec,
        out_shape=jax.ShapeDtypeStruct((n, d), F32),
        compiler_params=_params(("arbitrary",)),
        name="combine",
    )(entries, offsets, jmax, x1, cnt, ye, g)


def _route(aff_t, tb):
    e, n = aff_t.shape
    cap = CAPACITY_FACTOR * n // e
    gate, idx = lax.top_k(aff_t, cap)
    total = e * cap
    tok, src = lax.sort_key_val(idx.reshape(-1), jnp.arange(total, dtype=jnp.int32))
    first = jnp.searchsorted(tok, jnp.arange(n + 1, dtype=jnp.int32)).astype(jnp.int32)
    plane = jnp.arange(total, dtype=jnp.int32) - first[tok]
    dst = plane * tb + tok % tb
    entries = src | (dst << 16)
    cnt = first[1:] - first[:-1]
    offsets = first[::tb]
    jmax = jnp.max(cnt.reshape(n // tb, tb), axis=1)
    return gate, idx, entries, offsets, jmax, cnt.reshape(n, 1)


def _rope_tables(t):
    pos = jnp.arange(t)
    row = (pos // GRID_W).astype(F32)
    col = (pos % GRID_W).astype(F32)
    inv = ROPE_THETA ** (-jnp.arange(0, AXIS_DIM, 2, dtype=F32) / AXIS_DIM)
    ar, ac = row[:, None] * inv[None, :], col[:, None] * inv[None, :]
    cos = jnp.concatenate([jnp.cos(ar), jnp.cos(ar), jnp.cos(ac), jnp.cos(ac)], axis=-1)
    sin = jnp.concatenate([-jnp.sin(ar), jnp.sin(ar), -jnp.sin(ac), jnp.sin(ac)], axis=-1)
    return cos, sin


def _trunk(x, p):
    b, t, d = x.shape
    n = b * t
    assert N_EXPERTS * (CAPACITY_FACTOR * n // N_EXPERTS) <= 1 << 16
    cos, sin = _rope_tables(t)
    tk = _tile(t, 1024)
    tq = _tile(t, 128)
    tb = _tile(n, 128)
    xf = x.reshape(n, d)
    depth = p["w_in"].shape[0]
    for l in range(depth):
        proj = _inproj(xf, p["norm1_g"][l], p["w_in"][l]).reshape(b, t, IN_WIDTH_PAD)
        kr, vt = _kprep(proj, cos, sin, p["k_norm_g"][l], tk)
        att = _attention(proj, kr, vt, cos, sin, p["q_norm_g"][l], tq)
        qkm = _conv(proj, p["conv_w"][l], p["conv_b"][l])
        gates_t = jnp.swapaxes(proj[:, :, COL_MG:COL_MG + M_GATE_W], 1, 2)
        hfw = _mlstm(proj, qkm, gates_t, p["b_row"][l], p["b_col"][l], reverse=False)
        mem = _mlstm(proj, qkm, gates_t, p["b_row"][l], p["b_col"][l], reverse=True,
                     hfw=hfw, gn=p["mlstm_norm_g"][l])
        x1, h2, aff_t = _outproj(xf, mem.reshape(n, M_V_W), att.reshape(n, ATT_Q_W), p["w_out"][l],
                                 p["norm2_g"][l], p["w_router_t"][l])
        gate, idx, entries, offsets, jmax, cnt = _route(aff_t, tb)
        ye = _ffn(h2, idx, gate, p["w_gate"][l], p["w_up"][l], p["w_down"][l])
        xf = _combine(x1, ye, entries, offsets, jmax, cnt, p["final_norm_g"], tb,
                      final_norm=(l == depth - 1))
    return xf.reshape(b, t, d)


def _prepare(norm1_g, w_in, conv_w, conv_b, b_gates, q_norm_g, k_norm_g, mlstm_norm_g, w_out,
             norm2_g, w_router, w_gate, w_up, w_down, final_norm_g):
    depth, d, _ = w_in.shape
    return {
        "norm1_g": norm1_g.reshape(depth, 1, d),
        "w_in": jnp.pad(w_in.astype(BF16), ((0, 0), (0, 0), (0, IN_WIDTH_PAD - IN_WIDTH))),
        "conv_w": conv_w,
        "conv_b": conv_b.reshape(depth, 1, -1),
        "b_row": jnp.pad(b_gates, ((0, 0), (0, LANES - M_GATE_W))).reshape(depth, 1, LANES),
        "b_col": b_gates.reshape(depth, M_GATE_W, 1),
        "q_norm_g": q_norm_g.reshape(depth, 1, HEAD_DIM),
        "k_norm_g": k_norm_g.reshape(depth, 1, HEAD_DIM),
        "mlstm_norm_g": mlstm_norm_g.reshape(depth, 1, M_V_W),
        "w_out": w_out.astype(BF16),
        "norm2_g": norm2_g.reshape(depth, 1, d),
        "w_router_t": jnp.swapaxes(w_router, 1, 2),
        "w_gate": w_gate.astype(BF16),
        "w_up": w_up.astype(BF16),
        "w_down": w_down.astype(BF16),
        "final_norm_g": final_norm_g.reshape(1, d),
    }


def kernel(x_prompt, x_sample, norm1_g, w_in, conv_w, conv_b, b_gates, q_norm_g, k_norm_g, mlstm_norm_g,
           w_out, norm2_g, w_router, w_gate, w_up, w_down, final_norm_g):
    p = _prepare(norm1_g, w_in, conv_w, conv_b, b_gates, q_norm_g, k_norm_g, mlstm_norm_g, w_out,
                 norm2_g, w_router, w_gate, w_up, w_down, final_norm_g)
    return _trunk(x_prompt, p), _trunk(x_sample, p)
```

```python
import functools
import math

import jax
import jax.numpy as jnp
from jax import lax
from jax.experimental import pallas as pl
from jax.experimental.pallas import tpu as pltpu

F32 = jnp.float32
BF16 = jnp.bfloat16
EPS = 1e-6
LOG2E = 1.4426950408889634

LANES = 128
SUBLANES = 8
VMEM_BYTES_V7X = 64 * 1024 * 1024
VMEM_LIMIT = VMEM_BYTES_V7X - 8 * 1024 * 1024

GRID_W = 64
ATT_KV_HEADS = 2
ATT_GROUPS = 4
HEAD_DIM = 128
AXIS_DIM = HEAD_DIM // 2
ROPE_THETA = 10000.0
M_HEADS = 4
M_QK_DIM = 128
M_V_DIM = 256
M_CHUNK = 128
M_CONV = 5
N_GATE_KINDS = 4
N_EXPERTS = 16
CAPACITY_FACTOR = 2
ATT_Q_W = ATT_KV_HEADS * ATT_GROUPS * HEAD_DIM
ATT_KV_W = ATT_KV_HEADS * HEAD_DIM
M_QK_W = M_HEADS * M_QK_DIM
M_V_W = M_HEADS * M_V_DIM
M_GATE_W = N_GATE_KINDS * M_HEADS
COL_Q = 0
COL_K = COL_Q + ATT_Q_W
COL_V = COL_K + ATT_KV_W
COL_MQK = COL_V + ATT_KV_W
COL_MV = COL_MQK + 2 * M_QK_W
COL_MO = COL_MV + M_V_W
COL_MG = COL_MO + M_V_W
IN_WIDTH = COL_MG + M_GATE_W
IN_WIDTH_PAD = 4864
M_AUG = M_V_DIM + LANES
MAX_PLANES = N_EXPERTS

TILE_INPROJ = 256
TILE_ATT_Q = 128
TILE_ATT_KV = 1024
TILE_CONV = 512
TILE_OUTPROJ = 512
TILE_FFN_ROWS = 512
TILE_FFN_HIDDEN = 512
TILE_COMBINE = 128


def _tile(dim, pref):
    t = min(dim, pref)
    assert dim % t == 0, (dim, pref)
    return t


def _params(sem, vmem=VMEM_LIMIT):
    return pltpu.CompilerParams(dimension_semantics=sem, vmem_limit_bytes=vmem)


def _rms(x, g):
    return x * lax.rsqrt(jnp.mean(x * x, axis=-1, keepdims=True) + EPS) * g


def _log_sigmoid(x):
    return jnp.minimum(x, 0.0) - jnp.log1p(jnp.exp(-jnp.abs(x)))


def _rope(x, cos, sin):
    lane = lax.broadcasted_iota(jnp.int32, x.shape, 1)
    first = (lane % AXIS_DIM) < (AXIS_DIM // 2)
    swapped = jnp.where(first, pltpu.roll(x, LANES - AXIS_DIM // 2, axis=1),
                        pltpu.roll(x, AXIS_DIM // 2, axis=1))
    return x * cos + swapped * sin


def _inproj_kernel(x_ref, g_ref, w_ref, o_ref, *, col_chunk):
    h = _rms(x_ref[...], g_ref[...]).astype(BF16)
    width = o_ref.shape[1]
    for c0 in range(0, width, col_chunk):
        c1 = min(c0 + col_chunk, width)
        o_ref[:, c0:c1] = jnp.dot(h, w_ref[:, c0:c1], preferred_element_type=F32)


def _inproj(x, g, w):
    n, d = x.shape
    p = w.shape[1]
    tm = _tile(n, TILE_INPROJ)
    return pl.pallas_call(
        functools.partial(_inproj_kernel, col_chunk=512),
        grid=(n // tm,),
        in_specs=[pl.BlockSpec((tm, d), lambda i: (i, 0)),
                  pl.BlockSpec((1, d), lambda i: (0, 0)),
                  pl.BlockSpec((d, p), lambda i: (0, 0), pipeline_mode=pl.Buffered(1))],
        out_specs=pl.BlockSpec((tm, p), lambda i: (i, 0)),
        out_shape=jax.ShapeDtypeStruct((n, p), F32),
        compiler_params=_params(("parallel",)),
        name="inproj",
    )(x, g, w)


def _kprep_kernel(k_ref, v_ref, cos_ref, sin_ref, g_ref, ko_ref, vt_ref):
    cos, sin, g = cos_ref[...], sin_ref[...], g_ref[...]
    for h in range(ATT_KV_HEADS):
        k = _rms(k_ref[:, h * HEAD_DIM:(h + 1) * HEAD_DIM], g)
        ko_ref[h] = _rope(k, cos, sin).astype(BF16)
        vt_ref[h, 0] = v_ref[:, h * HEAD_DIM:(h + 1) * HEAD_DIM].T.astype(BF16)


def _kprep(proj, cos, sin, gk, tk):
    b, t, _ = proj.shape
    nkb = t // tk
    return pl.pallas_call(
        _kprep_kernel,
        grid=(b, nkb),
        in_specs=[pl.BlockSpec((None, tk, ATT_KV_W), lambda i, j: (i, j, COL_K // ATT_KV_W)),
                  pl.BlockSpec((None, tk, ATT_KV_W), lambda i, j: (i, j, COL_V // ATT_KV_W)),
                  pl.BlockSpec((tk, HEAD_DIM), lambda i, j: (j, 0)),
                  pl.BlockSpec((tk, HEAD_DIM), lambda i, j: (j, 0)),
                  pl.BlockSpec((1, HEAD_DIM), lambda i, j: (0, 0))],
        out_specs=[pl.BlockSpec((None, ATT_KV_HEADS, tk, HEAD_DIM), lambda i, j: (i, 0, j, 0)),
                   pl.BlockSpec((None, ATT_KV_HEADS, 1, HEAD_DIM, tk), lambda i, j: (i, 0, j, 0, 0))],
        out_shape=[jax.ShapeDtypeStruct((b, ATT_KV_HEADS, t, HEAD_DIM), BF16),
                   jax.ShapeDtypeStruct((b, ATT_KV_HEADS, nkb, HEAD_DIM, tk), BF16)],
        compiler_params=_params(("parallel", "parallel")),
        name="kprep",
    )(proj, proj, cos, sin, gk)


def _attn_kernel(q_ref, cos_ref, sin_ref, g_ref, k_ref, vt_ref, o_ref, s_ref, *, tq, tk, nkb):
    cos, sin, g = cos_ref[...], sin_ref[...], g_ref[...]
    qscale = HEAD_DIM ** -0.5 * LOG2E
    qs = []
    for i in range(ATT_GROUPS):
        q = _rms(q_ref[:, i * HEAD_DIM:(i + 1) * HEAD_DIM], g)
        qs.append((_rope(q, cos, sin) * qscale).astype(BF16))
    q4 = jnp.concatenate(qs, axis=0)
    nq = ATT_GROUPS * tq

    def scores(j):
        kb = k_ref[pl.ds(pl.multiple_of(j * tk, tk), tk), :]
        return lax.dot_general(kb, q4, (((1,), (1,)), ((), ())), preferred_element_type=F32)

    def softmax_step(j, slot, carry):
        m, l, acc = carry
        s = s_ref[slot]
        m_new = jnp.maximum(m, jnp.max(s, axis=0, keepdims=True))
        alpha = jnp.exp2(m - m_new)
        p = jnp.exp2(s - m_new)
        l = alpha * l + jnp.sum(p, axis=0, keepdims=True)
        acc = alpha * acc + jnp.dot(vt_ref[j], p.astype(BF16), preferred_element_type=F32)
        return m_new, l, acc

    def pair(jj, carry):
        j = 2 * jj
        s_ref[1] = scores(j + 1)
        carry = softmax_step(j, 0, carry)
        s_ref[0] = scores(j + 2)
        return softmax_step(j + 1, 1, carry)

    s_ref[0] = scores(0)
    carry = (jnp.full((1, nq), -jnp.inf, F32), jnp.zeros((1, nq), F32), jnp.zeros((HEAD_DIM, nq), F32))
    if nkb > 1:
        assert nkb % 2 == 0
        carry = lax.fori_loop(0, nkb // 2 - 1, pair, carry)
        s_ref[1] = scores(nkb - 1)
        carry = softmax_step(nkb - 2, 0, carry)
        carry = softmax_step(nkb - 1, 1, carry)
    else:
        carry = softmax_step(0, 0, carry)
    _, l, acc = carry
    o = acc / l
    for i in range(ATT_GROUPS):
        o_ref[:, i * HEAD_DIM:(i + 1) * HEAD_DIM] = o[:, i * tq:(i + 1) * tq].T.astype(BF16)


def _attention(proj, kr, vt, cos, sin, gq, tq):
    b, t, _ = proj.shape
    nkb, tk = vt.shape[2], vt.shape[4]
    gw = ATT_GROUPS * HEAD_DIM
    return pl.pallas_call(
        functools.partial(_attn_kernel, tq=tq, tk=tk, nkb=nkb),
        grid=(b, ATT_KV_HEADS, t // tq),
        in_specs=[pl.BlockSpec((None, tq, gw), lambda i, h, j: (i, j, h)),
                  pl.BlockSpec((tq, HEAD_DIM), lambda i, h, j: (j, 0)),
                  pl.BlockSpec((tq, HEAD_DIM), lambda i, h, j: (j, 0)),
                  pl.BlockSpec((1, HEAD_DIM), lambda i, h, j: (0, 0)),
                  pl.BlockSpec((None, None, t, HEAD_DIM), lambda i, h, j: (i, h, 0, 0)),
                  pl.BlockSpec((None, None, nkb, HEAD_DIM, tk), lambda i, h, j: (i, h, 0, 0, 0))],
        out_specs=pl.BlockSpec((None, tq, gw), lambda i, h, j: (i, j, h)),
        out_shape=jax.ShapeDtypeStruct((b, t, ATT_Q_W), BF16),
        scratch_shapes=[pltpu.VMEM((2, tk, ATT_GROUPS * tq), F32)],
        compiler_params=_params(("parallel", "parallel", "arbitrary")),
        name="attention",
    )(proj, cos, sin, gq, kr, vt)


def _conv_kernel(x_ref, prev_ref, next_ref, w_ref, b_ref, o_ref, pad_ref, *, tc, n_q_tiles):
    r = pl.program_id(1)
    nr = pl.num_programs(1)
    halo = SUBLANES
    pad_ref[0:halo, :] = jnp.where(r > 0, prev_ref[...], 0.0)
    pad_ref[halo:halo + tc, :] = x_ref[...]
    pad_ref[halo + tc:2 * halo + tc, :] = jnp.where(r < nr - 1, next_ref[...], 0.0)
    acc = jnp.broadcast_to(b_ref[...], x_ref.shape)
    for k in range(M_CONV):
        lo = halo - M_CONV // 2 + k
        acc = acc + w_ref[k:k + 1, :] * pad_ref[lo:lo + tc, :]
    y = acc * jax.nn.sigmoid(acc)
    scale = jnp.where(pl.program_id(2) >= n_q_tiles, M_QK_DIM ** -0.5, 1.0).astype(F32)
    o_ref[...] = (y * scale).astype(BF16)


def _conv(proj, w, bias):
    b, t, _ = proj.shape
    cw = M_QK_W
    tc = _tile(t, TILE_CONV)
    hb = tc // SUBLANES
    c0 = COL_MQK // cw
    last = t // SUBLANES - 1
    return pl.pallas_call(
        functools.partial(_conv_kernel, tc=tc, n_q_tiles=M_QK_W // cw),
        grid=(b, t // tc, 2 * M_QK_W // cw),
        in_specs=[pl.BlockSpec((None, tc, cw), lambda i, r, c: (i, r, c0 + c)),
                  pl.BlockSpec((None, SUBLANES, cw), lambda i, r, c: (i, jnp.maximum(r * hb - 1, 0), c0 + c)),
                  pl.BlockSpec((None, SUBLANES, cw), lambda i, r, c: (i, jnp.minimum((r + 1) * hb, last), c0 + c)),
                  pl.BlockSpec((M_CONV, cw), lambda i, r, c: (0, c)),
                  pl.BlockSpec((1, cw), lambda i, r, c: (0, c))],
        out_specs=pl.BlockSpec((None, tc, cw), lambda i, r, c: (i, r, c)),
        out_shape=jax.ShapeDtypeStruct((b, t, 2 * M_QK_W), BF16),
        scratch_shapes=[pltpu.VMEM((tc + 2 * SUBLANES, cw), F32)],
        compiler_params=_params(("parallel", "parallel", "parallel")),
        name="conv",
    )(proj, proj, proj, w, bias)


def _mlstm_kernel(*refs, reverse):
    if reverse:
        (q_ref, k_ref, vlo_ref, vhi_ref, gcol_ref, grow_ref, brow_ref, bcol_ref,
         hfw_ref, olo_ref, ohi_ref, gn_ref, out_ref, c_ref, m_ref) = refs
    else:
        (q_ref, k_ref, vlo_ref, vhi_ref, gcol_ref, grow_ref, brow_ref, bcol_ref,
         out_ref, c_ref, m_ref) = refs
    L = M_CHUNK

    @pl.when(pl.program_id(1) == 0)
    def _():
        c_ref[...] = jnp.zeros_like(c_ref)
        m_ref[...] = jnp.zeros_like(m_ref)

    ii = lax.broadcasted_iota(jnp.int32, (L, L), 0)
    jj = lax.broadcasted_iota(jnp.int32, (L, L), 1)
    mask = (jj >= ii) if reverse else (jj <= ii)
    tri = mask.astype(F32)
    tri_t = ((ii >= jj) if reverse else (ii <= jj)).astype(F32)
    kind = 2 * M_HEADS if reverse else 0

    gc = gcol_ref[...] + brow_ref[...]
    logf_c = _log_sigmoid(gc)
    cum_c = jnp.dot(tri, logf_c, preferred_element_type=F32, precision=lax.Precision.HIGHEST)
    tot_c = jnp.sum(logf_c, axis=0, keepdims=True)
    gr = grow_ref[...] + bcol_ref[...]
    logf_r = _log_sigmoid(gr)
    cum_r = jnp.dot(logf_r, tri_t, preferred_element_type=F32, precision=lax.Precision.HIGHEST)

    lane = lax.broadcasted_iota(jnp.int32, (L, LANES), 1)
    ones_col = jnp.where(lane == 0, 1.0, 0.0).astype(BF16)

    for hd in range(M_HEADS):
        li, lf = kind + hd, kind + M_HEADS + hd
        b_col, ig_col = cum_c[:, lf:lf + 1], gc[:, li:li + 1]
        b_row, ig_row = cum_r[lf:lf + 1, :], gr[li:li + 1, :]
        tot = tot_c[:, lf:lf + 1]
        m_prev = m_ref[hd:hd + 1, 0:1]
        q = q_ref[:, hd * M_QK_DIM:(hd + 1) * M_QK_DIM]
        k = k_ref[:, hd * M_QK_DIM:(hd + 1) * M_QK_DIM]
        v_ref = vlo_ref if hd < M_HEADS // 2 else vhi_ref
        vo = (hd % (M_HEADS // 2)) * M_V_DIM
        v_aug = jnp.concatenate([v_ref[:, vo:vo + M_V_DIM].astype(BF16), ones_col], axis=1)

        d = jnp.where(mask, b_col - (b_row - ig_row), -jnp.inf)
        inter = b_col + m_prev
        m_i = jnp.maximum(inter, jnp.max(d, axis=1, keepdims=True))
        w_intra = jnp.exp(d - m_i)
        s_inter = jnp.exp(inter - m_i)
        s = lax.dot_general(q, k, (((1,), (1,)), ((), ())), preferred_element_type=F32) * w_intra
        c_aug = c_ref[hd]
        num = (s_inter * jnp.dot(q, c_aug.astype(BF16), preferred_element_type=F32)
               + jnp.dot(s.astype(BF16), v_aug, preferred_element_type=F32))
        den = num[:, M_V_DIM:M_V_DIM + 1]
        h = num[:, :M_V_DIM] / jnp.maximum(jnp.abs(den), jnp.exp(-m_i))

        gvec = tot - b_col + ig_col
        m_new = jnp.maximum(tot + m_prev, jnp.max(gvec, axis=0, keepdims=True))
        s_prev = jnp.exp(tot + m_prev - m_new)
        kw = (k.astype(F32) * jnp.exp(gvec - m_new)).T.astype(BF16)
        c_ref[hd] = s_prev * c_aug + jnp.dot(kw, v_aug, preferred_element_type=F32)
        m_ref[hd:hd + 1, :] = jnp.broadcast_to(m_new, (1, LANES))

        cs = slice(hd * M_V_DIM, (hd + 1) * M_V_DIM)
        if reverse:
            o_ref = olo_ref if hd < M_HEADS // 2 else ohi_ref
            hs = hfw_ref[:, cs] + h
            y = _rms(hs, gn_ref[:, cs]) * jax.nn.sigmoid(o_ref[:, vo:vo + M_V_DIM])
            out_ref[:, cs] = y.astype(BF16)
        else:
            out_ref[:, cs] = h


def _mlstm(proj, qkm, gates_t, b_row, b_col, reverse, hfw=None, gn=None):
    b, t, _ = proj.shape
    nc = t // M_CHUNK
    half = M_V_W // 2

    def cidx(c):
        return nc - 1 - c if reverse else c

    in_specs = [pl.BlockSpec((None, M_CHUNK, M_QK_W), lambda i, c: (i, cidx(c), 0)),
                pl.BlockSpec((None, M_CHUNK, M_QK_W), lambda i, c: (i, cidx(c), 1)),
                pl.BlockSpec((None, M_CHUNK, half), lambda i, c: (i, cidx(c), COL_MV // half)),
                pl.BlockSpec((None, M_CHUNK, half), lambda i, c: (i, cidx(c), COL_MV // half + 1)),
                pl.BlockSpec((None, M_CHUNK, LANES), lambda i, c: (i, cidx(c), COL_MG // LANES)),
                pl.BlockSpec((None, M_GATE_W, M_CHUNK), lambda i, c: (i, 0, cidx(c))),
                pl.BlockSpec((1, LANES), lambda i, c: (0, 0)),
                pl.BlockSpec((M_GATE_W, 1), lambda i, c: (0, 0))]
    args = [qkm, qkm, proj, proj, proj, gates_t, b_row, b_col]
    if reverse:
        in_specs += [pl.BlockSpec((None, M_CHUNK, M_V_W), lambda i, c: (i, cidx(c), 0)),
                     pl.BlockSpec((None, M_CHUNK, half), lambda i, c: (i, cidx(c), COL_MO // half)),
                     pl.BlockSpec((None, M_CHUNK, half), lambda i, c: (i, cidx(c), COL_MO // half + 1)),
                     pl.BlockSpec((1, M_V_W), lambda i, c: (0, 0))]
        args += [hfw, proj, proj, gn]
    return pl.pallas_call(
        functools.partial(_mlstm_kernel, reverse=reverse),
        grid=(b, nc),
        in_specs=in_specs,
        out_specs=pl.BlockSpec((None, M_CHUNK, M_V_W), lambda i, c: (i, cidx(c), 0)),
        out_shape=jax.ShapeDtypeStruct((b, t, M_V_W), BF16 if reverse else F32),
        scratch_shapes=[pltpu.VMEM((M_HEADS, M_QK_DIM, M_AUG), F32),
                        pltpu.VMEM((SUBLANES, LANES), F32)],
        compiler_params=_params(("parallel", "arbitrary")),
        name="mlstm_bw" if reverse else "mlstm_fw",
    )(*args)


def _outproj_kernel(x_ref, mem_ref, att_ref, w_ref, g_ref, wr_ref, x1_ref, h_ref, aff_ref):
    half = mem_ref.shape[1]
    y = (jnp.dot(mem_ref[...], w_ref[0:half, :], preferred_element_type=F32)
         + jnp.dot(att_ref[...], w_ref[half:2 * half, :], preferred_element_type=F32))
    x1 = x_ref[...] + y
    x1_ref[...] = x1
    h = _rms(x1, g_ref[...])
    h_ref[...] = h
    logits = lax.dot_general(wr_ref[...], h, (((1,), (1,)), ((), ())), preferred_element_type=F32,
                             precision=lax.Precision.HIGHEST)
    e = jnp.exp(logits - jnp.max(logits, axis=0, keepdims=True))
    aff_ref[...] = e / jnp.sum(e, axis=0, keepdims=True)


def _outproj(x, mem, att, w, g, wr_t):
    n, d = x.shape
    half = mem.shape[1]
    tm = _tile(n, TILE_OUTPROJ)
    return pl.pallas_call(
        _outproj_kernel,
        grid=(n // tm,),
        in_specs=[pl.BlockSpec((tm, d), lambda i: (i, 0)),
                  pl.BlockSpec((tm, half), lambda i: (i, 0)),
                  pl.BlockSpec((tm, half), lambda i: (i, 0)),
                  pl.BlockSpec((2 * half, d), lambda i: (0, 0), pipeline_mode=pl.Buffered(1)),
                  pl.BlockSpec((1, d), lambda i: (0, 0)),
                  pl.BlockSpec((N_EXPERTS, d), lambda i: (0, 0))],
        out_specs=[pl.BlockSpec((tm, d), lambda i: (i, 0)),
                   pl.BlockSpec((tm, d), lambda i: (i, 0)),
                   pl.BlockSpec((N_EXPERTS, tm), lambda i: (0, i))],
        out_shape=[jax.ShapeDtypeStruct((n, d), F32),
                   jax.ShapeDtypeStruct((n, d), F32),
                   jax.ShapeDtypeStruct((N_EXPERTS, n), F32)],
        compiler_params=_params(("parallel",)),
        name="outproj_router",
    )(x, mem, att, w, g, wr_t)


def _row_copy(src_hbm, src_row, dst, dst_row, sem):
    return pltpu.make_async_copy(src_hbm.at[pl.ds(src_row, 1), :], dst.at[pl.ds(dst_row, 1), :], sem)


def _ffn_kernel(idx_ref, idx_next_ref, h_hbm, gate_ref, wg_ref, wu_ref, wd_ref, o_ref, xf_ref, xb_ref, sem,
                *, tm):
    nt, nf = pl.num_programs(1), pl.num_programs(2)
    f = pl.program_id(2)
    tile = pl.program_id(0) * nt + pl.program_id(1)
    slot = tile % 2
    rows_per_step = tm // nf

    def tile_wait(s):
        pltpu.make_async_copy(h_hbm.at[pl.ds(0, tm), :], xf_ref.at[s], sem.at[s]).wait()

    @pl.when(f == 0)
    def _():
        @pl.when(tile == 0)
        def _():
            def issue(r, carry):
                _row_copy(h_hbm, idx_ref[0, 0, r], xf_ref.at[0], r, sem.at[0]).start()
                return carry
            lax.fori_loop(0, tm, issue, 0)
        tile_wait(slot)
        xb_ref[...] = xf_ref[slot].astype(BF16)
        o_ref[...] = jnp.zeros_like(o_ref)

    for r in range(rows_per_step):
        row = f * rows_per_step + r
        _row_copy(h_hbm, idx_next_ref[0, 0, row], xf_ref.at[1 - slot], row, sem.at[1 - slot]).start()

    x = xb_ref[...]
    a = jnp.dot(x, wg_ref[...], preferred_element_type=F32)
    u = jnp.dot(x, wu_ref[...], preferred_element_type=F32)
    hid = (a * jax.nn.sigmoid(a) * u).astype(BF16)
    o_ref[...] += jnp.dot(hid, wd_ref[...], preferred_element_type=F32) * gate_ref[...]

    @pl.when((tile == pl.num_programs(0) * nt - 1) & (f == nf - 1))
    def _():
        tile_wait(1 - slot)


def _ffn(h, idx, gate, wg, wu, wd):
    n, d = h.shape
    e, cap = idx.shape
    ff = wg.shape[2]
    tm = _tile(cap, TILE_FFN_ROWS)
    tf = _tile(ff, TILE_FFN_HIDDEN)
    nt = cap // tm
    assert tm % (ff // tf) == 0
    idx3 = idx.reshape(e * nt, 1, tm)
    gate2 = gate.reshape(e * cap, 1)
    last = e * nt - 1
    return pl.pallas_call(
        functools.partial(_ffn_kernel, tm=tm),
        grid=(e, nt, ff // tf),
        in_specs=[pl.BlockSpec((1, 1, tm), lambda i, j, f: (i * nt + j, 0, 0), memory_space=pltpu.SMEM),
                  pl.BlockSpec((1, 1, tm), lambda i, j, f: (jnp.minimum(i * nt + j + 1, last), 0, 0),
                               memory_space=pltpu.SMEM),
                  pl.BlockSpec(memory_space=pl.ANY),
                  pl.BlockSpec((tm, 1), lambda i, j, f: (i * nt + j, 0)),
                  pl.BlockSpec((None, d, tf), lambda i, j, f: (i, 0, f)),
                  pl.BlockSpec((None, d, tf), lambda i, j, f: (i, 0, f)),
                  pl.BlockSpec((None, tf, d), lambda i, j, f: (i, f, 0))],
        out_specs=pl.BlockSpec((tm, d), lambda i, j, f: (i * nt + j, 0)),
        out_shape=jax.ShapeDtypeStruct((e * cap, d), F32),
        scratch_shapes=[pltpu.VMEM((2, tm, d), F32), pltpu.VMEM((tm, d), BF16), pltpu.SemaphoreType.DMA((2,))],
        compiler_params=_params(("arbitrary", "arbitrary", "arbitrary")),
        name="expert_ffn",
    )(idx3, idx3, h, gate2, wg, wu, wd)


WAIT_CHUNK = 256
COMBINE_COLS = 256
ISSUE_UNROLL = 8


def _combine_kernel(ent_ref, off_ref, jmax_ref, x_ref, ye_hbm, g_ref, o_ref, buf_ref, sem, *, tb, final_norm):
    t = pl.program_id(0)
    d = buf_ref.shape[2]
    slot = t % 2

    def fill(tile, s):
        for j in range(MAX_PLANES):
            @pl.when(j < jmax_ref[tile])
            def _():
                buf_ref[s, j * tb:(j + 1) * tb, :] = jnp.zeros((tb, d), F32)
        lo = off_ref[tile]
        n_ent = off_ref[tile + 1] - lo

        def issue(p):
            ent = ent_ref[p]
            _row_copy(ye_hbm, ent & 0xFFFF, buf_ref.at[s], ent >> 16, sem.at[s]).start()

        def issue_group(q, carry):
            for r in range(ISSUE_UNROLL):
                issue(lo + q * ISSUE_UNROLL + r)
            return carry
        groups = n_ent // ISSUE_UNROLL
        lax.fori_loop(0, groups, issue_group, 0)

        def issue_rest(p, carry):
            issue(p)
            return carry
        lax.fori_loop(lo + groups * ISSUE_UNROLL, lo + n_ent, issue_rest, 0)

    @pl.when(t == 0)
    def _():
        fill(0, 0)

    @pl.when(t + 1 < pl.num_programs(0))
    def _():
        fill(t + 1, 1 - slot)

    def wait_rows(k):
        pltpu.make_async_copy(ye_hbm.at[pl.ds(0, k), :], buf_ref.at[slot, pl.ds(0, k), :], sem.at[slot]).wait()

    count = off_ref[t + 1] - off_ref[t]

    def bulk(p, carry):
        wait_rows(WAIT_CHUNK)
        return carry
    lax.fori_loop(0, count // WAIT_CHUNK, bulk, 0)
    k = WAIT_CHUNK // 2
    while k >= 1:
        @pl.when((count & k) != 0)
        def _():
            wait_rows(k)
        k //= 2

    jmax = jmax_ref[t]
    for c0 in range(0, d, COMBINE_COLS):
        cols = slice(c0, min(c0 + COMBINE_COLS, d))

        def add(j, acc):
            return acc + buf_ref[slot, pl.ds(pl.multiple_of(j * tb, tb), tb), cols]
        o_ref[:, cols] = lax.fori_loop(0, jmax, add, x_ref[:, cols])

    if final_norm:
        o_ref[...] = _rms(o_ref[...], g_ref[...])


def _combine(x1, ye, entries, offsets, jmax, g, tb, final_norm):
    n, d = x1.shape
    assert ye.shape[0] >= WAIT_CHUNK and MAX_PLANES * tb >= WAIT_CHUNK
    grid_spec = pltpu.PrefetchScalarGridSpec(
        num_scalar_prefetch=3,
        grid=(n // tb,),
        in_specs=[pl.BlockSpec((tb, d), lambda i, *_: (i, 0)),
                  pl.BlockSpec(memory_space=pl.ANY),
                  pl.BlockSpec((1, d), lambda i, *_: (0, 0))],
        out_specs=pl.BlockSpec((tb, d), lambda i, *_: (i, 0)),
        scratch_shapes=[pltpu.VMEM((2, MAX_PLANES * tb, d), F32), pltpu.SemaphoreType.DMA((2,))],
    )
    return pl.pallas_call(
        functools.partial(_combine_kernel, tb=tb, final_norm=final_norm),
        grid_spec=grid_spec,
        out_shape=jax.ShapeDtypeStruct((n, d), F32),
        compiler_params=_params(("arbitrary",)),
        name="combine",
    )(entries, offsets, jmax, x1, ye, g)


def _route(aff_t, tb):
    e, n = aff_t.shape
    cap = CAPACITY_FACTOR * n // e
    gate, idx = lax.top_k(aff_t, cap)
    total = e * cap
    pos = jnp.arange(total, dtype=jnp.int32)
    tok, src = lax.sort_key_val(idx.reshape(-1), pos)
    new_tok = jnp.concatenate([jnp.ones((1,), bool), tok[1:] != tok[:-1]])
    plane = pos - lax.cummax(jnp.where(new_tok, pos, 0))
    entries = src | ((plane * tb + tok % tb) << 16)
    tiles = jnp.arange(n // tb + 1, dtype=jnp.int32)
    offsets = jnp.sum(tok[None, :] < (tiles * tb)[:, None], axis=1, dtype=jnp.int32)
    in_tile = (tok // tb)[None, :] == tiles[:-1, None]
    jmax = jnp.max(jnp.where(in_tile, plane[None, :] + 1, 0), axis=1)
    return gate, idx, entries, offsets, jmax


def _rope_tables(t):
    pos = jnp.arange(t)
    row = (pos // GRID_W).astype(F32)
    col = (pos % GRID_W).astype(F32)
    inv = ROPE_THETA ** (-jnp.arange(0, AXIS_DIM, 2, dtype=F32) / AXIS_DIM)
    ar, ac = row[:, None] * inv[None, :], col[:, None] * inv[None, :]
    cos = jnp.concatenate([jnp.cos(ar), jnp.cos(ar), jnp.cos(ac), jnp.cos(ac)], axis=-1)
    sin = jnp.concatenate([-jnp.sin(ar), jnp.sin(ar), -jnp.sin(ac), jnp.sin(ac)], axis=-1)
    return cos, sin


def _trunk(x, p):
    b, t, d = x.shape
    n = b * t
    assert N_EXPERTS * (CAPACITY_FACTOR * n // N_EXPERTS) <= 1 << 16
    cos, sin = _rope_tables(t)
    tk = _tile(t, TILE_ATT_KV)
    tq = _tile(t, TILE_ATT_Q)
    tb = _tile(n, TILE_COMBINE)
    xf = x.reshape(n, d)
    depth = p["w_in"].shape[0]
    for l in range(depth):
        proj = _inproj(xf, p["norm1_g"][l], p["w_in"][l]).reshape(b, t, IN_WIDTH_PAD)
        kr, vt = _kprep(proj, cos, sin, p["k_norm_g"][l], tk)
        att = _attention(proj, kr, vt, cos, sin, p["q_norm_g"][l], tq)
        qkm = _conv(proj, p["conv_w"][l], p["conv_b"][l])
        gates_t = jnp.swapaxes(proj[:, :, COL_MG:COL_MG + M_GATE_W], 1, 2)
        hfw = _mlstm(proj, qkm, gates_t, p["b_row"][l], p["b_col"][l], reverse=False)
        mem = _mlstm(proj, qkm, gates_t, p["b_row"][l], p["b_col"][l], reverse=True,
                     hfw=hfw, gn=p["mlstm_norm_g"][l])
        x1, h2, aff_t = _outproj(xf, mem.reshape(n, M_V_W), att.reshape(n, ATT_Q_W), p["w_out"][l],
                                 p["norm2_g"][l], p["w_router_t"][l])
        gate, idx, entries, offsets, jmax = _route(aff_t, tb)
        ye = _ffn(h2, idx, gate, p["w_gate"][l], p["w_up"][l], p["w_down"][l])
        xf = _combine(x1, ye, entries, offsets, jmax, p["final_norm_g"], tb,
                      final_norm=(l == depth - 1))
    return xf.reshape(b, t, d)


def _prepare(norm1_g, w_in, conv_w, conv_b, b_gates, q_norm_g, k_norm_g, mlstm_norm_g, w_out,
             norm2_g, w_router, w_gate, w_up, w_down, final_norm_g):
    depth, d, _ = w_in.shape
    return {
        "norm1_g": norm1_g.reshape(depth, 1, d),
        "w_in": jnp.pad(w_in.astype(BF16), ((0, 0), (0, 0), (0, IN_WIDTH_PAD - IN_WIDTH))),
        "conv_w": conv_w,
        "conv_b": conv_b.reshape(depth, 1, -1),
        "b_row": jnp.pad(b_gates, ((0, 0), (0, LANES - M_GATE_W))).reshape(depth, 1, LANES),
        "b_col": b_gates.reshape(depth, M_GATE_W, 1),
        "q_norm_g": q_norm_g.reshape(depth, 1, HEAD_DIM),
        "k_norm_g": k_norm_g.reshape(depth, 1, HEAD_DIM),
        "mlstm_norm_g": mlstm_norm_g.reshape(depth, 1, M_V_W),
        "w_out": w_out.astype(BF16),
        "norm2_g": norm2_g.reshape(depth, 1, d),
        "w_router_t": jnp.swapaxes(w_router, 1, 2),
        "w_gate": w_gate.astype(BF16),
        "w_up": w_up.astype(BF16),
        "w_down": w_down.astype(BF16),
        "final_norm_g": final_norm_g.reshape(1, d),
    }


def kernel(x_prompt, x_sample, norm1_g, w_in, conv_w, conv_b, b_gates, q_norm_g, k_norm_g, mlstm_norm_g,
           w_out, norm2_g, w_router, w_gate, w_up, w_down, final_norm_g):
    p = _prepare(norm1_g, w_in, conv_w, conv_b, b_gates, q_norm_g, k_norm_g, mlstm_norm_g, w_out,
                 norm2_g, w_router, w_gate, w_up, w_down, final_norm_g)
    return _trunk(x_prompt, p), _trunk(x_sample, p)
```

```python
import functools
import math

import jax
import jax.numpy as jnp
from jax import lax
from jax.experimental import pallas as pl
from jax.experimental.pallas import tpu as pltpu

F32 = jnp.float32
BF16 = jnp.bfloat16
EPS = 1e-6
LOG2E = 1.4426950408889634

LANES = 128
SUBLANES = 8
VMEM_BYTES_V7X = 64 * 1024 * 1024
VMEM_LIMIT = VMEM_BYTES_V7X - 8 * 1024 * 1024

GRID_W = 64
ATT_KV_HEADS = 2
ATT_GROUPS = 4
HEAD_DIM = 128
AXIS_DIM = HEAD_DIM // 2
ROPE_THETA = 10000.0
M_HEADS = 4
M_QK_DIM = 128
M_V_DIM = 256
M_CHUNK = 128
M_CONV = 5
N_GATE_KINDS = 4
N_EXPERTS = 16
CAPACITY_FACTOR = 2
ATT_Q_W = ATT_KV_HEADS * ATT_GROUPS * HEAD_DIM
ATT_KV_W = ATT_KV_HEADS * HEAD_DIM
M_QK_W = M_HEADS * M_QK_DIM
M_V_W = M_HEADS * M_V_DIM
M_GATE_W = N_GATE_KINDS * M_HEADS
COL_Q = 0
COL_K = COL_Q + ATT_Q_W
COL_V = COL_K + ATT_KV_W
COL_MQK = COL_V + ATT_KV_W
COL_MV = COL_MQK + 2 * M_QK_W
COL_MO = COL_MV + M_V_W
COL_MG = COL_MO + M_V_W
IN_WIDTH = COL_MG + M_GATE_W
IN_WIDTH_PAD = 4864
M_AUG = M_V_DIM + LANES
MAX_PLANES = N_EXPERTS

TILE_INPROJ = 256
TILE_ATT_Q = 256
TILE_ATT_KV = 1024
TILE_CONV = 512
TILE_OUTPROJ = 512
TILE_FFN_ROWS = 512
TILE_FFN_HIDDEN = 512
TILE_COMBINE = 128
MLSTM_SEQS_PER_STEP = 2


def _tile(dim, pref):
    t = min(dim, pref)
    assert dim % t == 0, (dim, pref)
    return t


def _params(sem, vmem=VMEM_LIMIT):
    return pltpu.CompilerParams(dimension_semantics=sem, vmem_limit_bytes=vmem)


def _rms(x, g):
    return x * lax.rsqrt(jnp.mean(x * x, axis=-1, keepdims=True) + EPS) * g


def _log_sigmoid(x):
    return jnp.minimum(x, 0.0) - jnp.log1p(jnp.exp(-jnp.abs(x)))


def _rope(x, cos, sin):
    lane = lax.broadcasted_iota(jnp.int32, x.shape, 1)
    first = (lane % AXIS_DIM) < (AXIS_DIM // 2)
    swapped = jnp.where(first, pltpu.roll(x, LANES - AXIS_DIM // 2, axis=1),
                        pltpu.roll(x, AXIS_DIM // 2, axis=1))
    return x * cos + swapped * sin


def _inproj_kernel(x_ref, g_ref, w_ref, o_ref, gt_ref, *, col_chunk):
    h = _rms(x_ref[...], g_ref[...]).astype(BF16)
    width = o_ref.shape[1]
    for c0 in range(0, width, col_chunk):
        c1 = min(c0 + col_chunk, width)
        o_ref[:, c0:c1] = jnp.dot(h, w_ref[:, c0:c1], preferred_element_type=F32)
    gt_ref[...] = o_ref[:, COL_MG:COL_MG + LANES].T[:M_GATE_W, :]


def _inproj(x, g, w):
    n, d = x.shape
    p = w.shape[1]
    tm = _tile(n, TILE_INPROJ)
    return pl.pallas_call(
        functools.partial(_inproj_kernel, col_chunk=512),
        grid=(n // tm,),
        in_specs=[pl.BlockSpec((tm, d), lambda i: (i, 0)),
                  pl.BlockSpec((1, d), lambda i: (0, 0)),
                  pl.BlockSpec((d, p), lambda i: (0, 0), pipeline_mode=pl.Buffered(1))],
        out_specs=[pl.BlockSpec((tm, p), lambda i: (i, 0)),
                   pl.BlockSpec((M_GATE_W, tm), lambda i: (0, i))],
        out_shape=[jax.ShapeDtypeStruct((n, p), F32),
                   jax.ShapeDtypeStruct((M_GATE_W, n), F32)],
        compiler_params=_params(("parallel",)),
        name="inproj",
    )(x, g, w)


def _kprep_kernel(k_ref, v_ref, cos_ref, sin_ref, g_ref, ko_ref, vt_ref):
    cos, sin, g = cos_ref[...], sin_ref[...], g_ref[...]
    for h in range(ATT_KV_HEADS):
        k = _rms(k_ref[:, h * HEAD_DIM:(h + 1) * HEAD_DIM], g)
        ko_ref[h] = _rope(k, cos, sin).astype(BF16)
        vt_ref[h, 0] = v_ref[:, h * HEAD_DIM:(h + 1) * HEAD_DIM].T.astype(BF16)


def _kprep(proj, cos, sin, gk, tk):
    b, t, _ = proj.shape
    nkb = t // tk
    return pl.pallas_call(
        _kprep_kernel,
        grid=(b, nkb),
        in_specs=[pl.BlockSpec((None, tk, ATT_KV_W), lambda i, j: (i, j, COL_K // ATT_KV_W)),
                  pl.BlockSpec((None, tk, ATT_KV_W), lambda i, j: (i, j, COL_V // ATT_KV_W)),
                  pl.BlockSpec((tk, HEAD_DIM), lambda i, j: (j, 0)),
                  pl.BlockSpec((tk, HEAD_DIM), lambda i, j: (j, 0)),
                  pl.BlockSpec((1, HEAD_DIM), lambda i, j: (0, 0))],
        out_specs=[pl.BlockSpec((None, ATT_KV_HEADS, tk, HEAD_DIM), lambda i, j: (i, 0, j, 0)),
                   pl.BlockSpec((None, ATT_KV_HEADS, 1, HEAD_DIM, tk), lambda i, j: (i, 0, j, 0, 0))],
        out_shape=[jax.ShapeDtypeStruct((b, ATT_KV_HEADS, t, HEAD_DIM), BF16),
                   jax.ShapeDtypeStruct((b, ATT_KV_HEADS, nkb, HEAD_DIM, tk), BF16)],
        compiler_params=_params(("parallel", "parallel")),
        name="kprep",
    )(proj, proj, cos, sin, gk)


def _attn_kernel(q_ref, cos_ref, sin_ref, g_ref, k_ref, vt_ref, o_ref, s_ref, *, tq, tk, nkb):
    cos, sin, g = cos_ref[...], sin_ref[...], g_ref[...]
    qscale = HEAD_DIM ** -0.5 * LOG2E
    qs = []
    for i in range(ATT_GROUPS):
        q = _rms(q_ref[:, i * HEAD_DIM:(i + 1) * HEAD_DIM], g)
        qs.append((_rope(q, cos, sin) * qscale).astype(BF16))
    q4 = jnp.concatenate(qs, axis=0)
    nq = ATT_GROUPS * tq

    def scores(j):
        kb = k_ref[pl.ds(pl.multiple_of(j * tk, tk), tk), :]
        return lax.dot_general(kb, q4, (((1,), (1,)), ((), ())), preferred_element_type=F32)

    def softmax_step(j, slot, carry):
        m, l, acc = carry
        s = s_ref[slot]
        m_new = jnp.maximum(m, jnp.max(s, axis=0, keepdims=True))
        alpha = jnp.exp2(m - m_new)
        p = jnp.exp2(s - m_new)
        l = alpha * l + jnp.sum(p, axis=0, keepdims=True)
        acc = alpha * acc + jnp.dot(vt_ref[j], p.astype(BF16), preferred_element_type=F32)
        return m_new, l, acc

    def pair(jj, carry):
        j = 2 * jj
        s_ref[1] = scores(j + 1)
        carry = softmax_step(j, 0, carry)
        s_ref[0] = scores(j + 2)
        return softmax_step(j + 1, 1, carry)

    s_ref[0] = scores(0)
    carry = (jnp.full((1, nq), -jnp.inf, F32), jnp.zeros((1, nq), F32), jnp.zeros((HEAD_DIM, nq), F32))
    if nkb > 1:
        assert nkb % 2 == 0
        carry = lax.fori_loop(0, nkb // 2 - 1, pair, carry)
        s_ref[1] = scores(nkb - 1)
        carry = softmax_step(nkb - 2, 0, carry)
        carry = softmax_step(nkb - 1, 1, carry)
    else:
        carry = softmax_step(0, 0, carry)
    _, l, acc = carry
    o = acc / l
    for i in range(ATT_GROUPS):
        o_ref[:, i * HEAD_DIM:(i + 1) * HEAD_DIM] = o[:, i * tq:(i + 1) * tq].T.astype(BF16)


def _attention(proj, kr, vt, cos, sin, gq, tq):
    b, t, _ = proj.shape
    nkb, tk = vt.shape[2], vt.shape[4]
    gw = ATT_GROUPS * HEAD_DIM
    return pl.pallas_call(
        functools.partial(_attn_kernel, tq=tq, tk=tk, nkb=nkb),
        grid=(b, ATT_KV_HEADS, t // tq),
        in_specs=[pl.BlockSpec((None, tq, gw), lambda i, h, j: (i, j, h)),
                  pl.BlockSpec((tq, HEAD_DIM), lambda i, h, j: (j, 0)),
                  pl.BlockSpec((tq, HEAD_DIM), lambda i, h, j: (j, 0)),
                  pl.BlockSpec((1, HEAD_DIM), lambda i, h, j: (0, 0)),
                  pl.BlockSpec((None, None, t, HEAD_DIM), lambda i, h, j: (i, h, 0, 0)),
                  pl.BlockSpec((None, None, nkb, HEAD_DIM, tk), lambda i, h, j: (i, h, 0, 0, 0))],
        out_specs=pl.BlockSpec((None, tq, gw), lambda i, h, j: (i, j, h)),
        out_shape=jax.ShapeDtypeStruct((b, t, ATT_Q_W), BF16),
        scratch_shapes=[pltpu.VMEM((2, tk, ATT_GROUPS * tq), F32)],
        compiler_params=_params(("parallel", "parallel", "arbitrary")),
        name="attention",
    )(proj, cos, sin, gq, kr, vt)


def _conv_kernel(x_ref, prev_ref, next_ref, w_ref, b_ref, o_ref, pad_ref, *, tc, n_q_tiles):
    r = pl.program_id(1)
    nr = pl.num_programs(1)
    halo = SUBLANES
    pad_ref[0:halo, :] = jnp.where(r > 0, prev_ref[...], 0.0)
    pad_ref[halo:halo + tc, :] = x_ref[...]
    pad_ref[halo + tc:2 * halo + tc, :] = jnp.where(r < nr - 1, next_ref[...], 0.0)
    acc = jnp.broadcast_to(b_ref[...], x_ref.shape)
    for k in range(M_CONV):
        lo = halo - M_CONV // 2 + k
        acc = acc + w_ref[k:k + 1, :] * pad_ref[lo:lo + tc, :]
    y = acc * jax.nn.sigmoid(acc)
    scale = jnp.where(pl.program_id(2) >= n_q_tiles, M_QK_DIM ** -0.5, 1.0).astype(F32)
    o_ref[...] = (y * scale).astype(BF16)


def _conv(proj, w, bias):
    b, t, _ = proj.shape
    cw = M_QK_W
    tc = _tile(t, TILE_CONV)
    hb = tc // SUBLANES
    c0 = COL_MQK // cw
    last = t // SUBLANES - 1
    return pl.pallas_call(
        functools.partial(_conv_kernel, tc=tc, n_q_tiles=M_QK_W // cw),
        grid=(b, t // tc, 2 * M_QK_W // cw),
        in_specs=[pl.BlockSpec((None, tc, cw), lambda i, r, c: (i, r, c0 + c)),
                  pl.BlockSpec((None, SUBLANES, cw), lambda i, r, c: (i, jnp.maximum(r * hb - 1, 0), c0 + c)),
                  pl.BlockSpec((None, SUBLANES, cw), lambda i, r, c: (i, jnp.minimum((r + 1) * hb, last), c0 + c)),
                  pl.BlockSpec((M_CONV, cw), lambda i, r, c: (0, c)),
                  pl.BlockSpec((1, cw), lambda i, r, c: (0, c))],
        out_specs=pl.BlockSpec((None, tc, cw), lambda i, r, c: (i, r, c)),
        out_shape=jax.ShapeDtypeStruct((b, t, 2 * M_QK_W), BF16),
        scratch_shapes=[pltpu.VMEM((tc + 2 * SUBLANES, cw), F32)],
        compiler_params=_params(("parallel", "parallel", "parallel")),
        name="conv",
    )(proj, proj, proj, w, bias)


def _mlstm_kernel(*refs, reverse, bb):
    if reverse:
        (q_all, k_all, vlo_ref, vhi_ref, gcol_ref, grow_ref, brow_ref, bcol_ref,
         hfw_all, olo_ref, ohi_ref, gn_ref, out_all, c_ref, m_ref) = refs
    else:
        (q_all, k_all, vlo_ref, vhi_ref, gcol_ref, grow_ref, brow_ref, bcol_ref,
         out_all, c_ref, m_ref) = refs
    L = M_CHUNK

    @pl.when(pl.program_id(1) == 0)
    def _():
        c_ref[...] = jnp.zeros_like(c_ref)
        m_ref[...] = jnp.zeros_like(m_ref)

    ii = lax.broadcasted_iota(jnp.int32, (L, L), 0)
    jj = lax.broadcasted_iota(jnp.int32, (L, L), 1)
    mask = (jj >= ii) if reverse else (jj <= ii)
    tri = mask.astype(F32)
    tri_t = ((ii >= jj) if reverse else (ii <= jj)).astype(F32)
    kind = 2 * M_HEADS if reverse else 0
    lane = lax.broadcasted_iota(jnp.int32, (L, LANES), 1)
    ones_col = jnp.where(lane == 0, 1.0, 0.0).astype(BF16)

    m_all = m_ref[...]
    m_out = []
    for st in range(bb * M_HEADS):
        bi, hd = divmod(st, M_HEADS)
        if hd == 0:
            gc = gcol_ref[bi] + brow_ref[...]
            logf_c = _log_sigmoid(gc)
            cum_c = jnp.dot(tri, logf_c, preferred_element_type=F32, precision=lax.Precision.HIGHEST)
            tot_c = jnp.sum(logf_c, axis=0, keepdims=True)
            gr = grow_ref[bi] + bcol_ref[...]
            logf_r = _log_sigmoid(gr)
            cum_r = jnp.dot(logf_r, tri_t, preferred_element_type=F32, precision=lax.Precision.HIGHEST)
        q_ref, k_ref, hfw_ref = q_all.at[bi], k_all.at[bi], (hfw_all.at[bi] if reverse else None)
        out_ref = out_all.at[bi]
        li, lf = kind + hd, kind + M_HEADS + hd
        b_col, ig_col = cum_c[:, lf:lf + 1], gc[:, li:li + 1]
        b_row, ig_row = cum_r[lf:lf + 1, :], gr[li:li + 1, :]
        tot = tot_c[:, lf:lf + 1]
        m_prev = m_all[st:st + 1, 0:1]
        q = q_ref[:, hd * M_QK_DIM:(hd + 1) * M_QK_DIM]
        k = k_ref[:, hd * M_QK_DIM:(hd + 1) * M_QK_DIM]
        v_ref = (vlo_ref if hd < M_HEADS // 2 else vhi_ref).at[bi]
        vo = (hd % (M_HEADS // 2)) * M_V_DIM
        v_aug = jnp.concatenate([v_ref[:, vo:vo + M_V_DIM].astype(BF16), ones_col], axis=1)

        d = jnp.where(mask, b_col - (b_row - ig_row), -jnp.inf)
        inter = b_col + m_prev
        m_i = jnp.maximum(inter, jnp.max(d, axis=1, keepdims=True))
        w_intra = jnp.exp(d - m_i)
        s_inter = jnp.exp(inter - m_i)
        s = lax.dot_general(q, k, (((1,), (1,)), ((), ())), preferred_element_type=F32) * w_intra
        c_aug = c_ref[st]
        num = (s_inter * jnp.dot(q, c_aug.astype(BF16), preferred_element_type=F32)
               + jnp.dot(s.astype(BF16), v_aug, preferred_element_type=F32))
        den = num[:, M_V_DIM:M_V_DIM + 1]
        h = num[:, :M_V_DIM] / jnp.maximum(jnp.abs(den), jnp.exp(-m_i))

        gvec = tot - b_col + ig_col
        m_new = jnp.maximum(tot + m_prev, jnp.max(gvec, axis=0, keepdims=True))
        s_prev = jnp.exp(tot + m_prev - m_new)
        kw = (k.astype(F32) * jnp.exp(gvec - m_new)).T.astype(BF16)
        c_ref[st] = s_prev * c_aug + jnp.dot(kw, v_aug, preferred_element_type=F32)
        m_out.append(jnp.broadcast_to(m_new, (1, LANES)))

        cs = slice(hd * M_V_DIM, (hd + 1) * M_V_DIM)
        if reverse:
            o_ref = (olo_ref if hd < M_HEADS // 2 else ohi_ref).at[bi]
            hs = hfw_ref[:, cs] + h
            y = _rms(hs, gn_ref[:, cs]) * jax.nn.sigmoid(o_ref[:, vo:vo + M_V_DIM])
            out_ref[:, cs] = y.astype(BF16)
        else:
            out_ref[:, cs] = h
    m_out.append(jnp.zeros((m_ref.shape[0] - len(m_out), LANES), F32))
    m_ref[...] = jnp.concatenate([m for m in m_out if m.shape[0] > 0], axis=0)


def _mlstm(proj, qkm, gates_t, b_row, b_col, reverse, hfw=None, gn=None):
    b, t, _ = proj.shape
    nc = t // M_CHUNK
    half = M_V_W // 2
    bb = _tile(b, MLSTM_SEQS_PER_STEP)
    n_state = bb * M_HEADS

    def cidx(c):
        return nc - 1 - c if reverse else c

    in_specs = [pl.BlockSpec((bb, M_CHUNK, M_QK_W), lambda i, c: (i, cidx(c), 0)),
                pl.BlockSpec((bb, M_CHUNK, M_QK_W), lambda i, c: (i, cidx(c), 1)),
                pl.BlockSpec((bb, M_CHUNK, half), lambda i, c: (i, cidx(c), COL_MV // half)),
                pl.BlockSpec((bb, M_CHUNK, half), lambda i, c: (i, cidx(c), COL_MV // half + 1)),
                pl.BlockSpec((bb, M_CHUNK, LANES), lambda i, c: (i, cidx(c), COL_MG // LANES)),
                pl.BlockSpec((bb, M_GATE_W, M_CHUNK), lambda i, c: (i, 0, cidx(c))),
                pl.BlockSpec((1, LANES), lambda i, c: (0, 0)),
                pl.BlockSpec((M_GATE_W, 1), lambda i, c: (0, 0))]
    args = [qkm, qkm, proj, proj, proj, gates_t, b_row, b_col]
    if reverse:
        in_specs += [pl.BlockSpec((bb, M_CHUNK, M_V_W), lambda i, c: (i, cidx(c), 0)),
                     pl.BlockSpec((bb, M_CHUNK, half), lambda i, c: (i, cidx(c), COL_MO // half)),
                     pl.BlockSpec((bb, M_CHUNK, half), lambda i, c: (i, cidx(c), COL_MO // half + 1)),
                     pl.BlockSpec((1, M_V_W), lambda i, c: (0, 0))]
        args += [hfw, proj, proj, gn]
    return pl.pallas_call(
        functools.partial(_mlstm_kernel, reverse=reverse, bb=bb),
        grid=(b // bb, nc),
        in_specs=in_specs,
        out_specs=pl.BlockSpec((bb, M_CHUNK, M_V_W), lambda i, c: (i, cidx(c), 0)),
        out_shape=jax.ShapeDtypeStruct((b, t, M_V_W), BF16 if reverse else F32),
        scratch_shapes=[pltpu.VMEM((n_state, M_QK_DIM, M_AUG), F32),
                        pltpu.VMEM((-(-n_state // SUBLANES) * SUBLANES, LANES), F32)],
        compiler_params=_params(("parallel", "arbitrary")),
        name="mlstm_bw" if reverse else "mlstm_fw",
    )(*args)


def _outproj_kernel(x_ref, mem_ref, att_ref, w_ref, g_ref, wr_ref, x1_ref, h_ref, aff_ref):
    half = mem_ref.shape[1]
    y = (jnp.dot(mem_ref[...], w_ref[0:half, :], preferred_element_type=F32)
         + jnp.dot(att_ref[...], w_ref[half:2 * half, :], preferred_element_type=F32))
    x1 = x_ref[...] + y
    x1_ref[...] = x1
    h = _rms(x1, g_ref[...])
    h_ref[...] = h
    logits = lax.dot_general(wr_ref[...], h, (((1,), (1,)), ((), ())), preferred_element_type=F32,
                             precision=lax.Precision.HIGHEST)
    e = jnp.exp(logits - jnp.max(logits, axis=0, keepdims=True))
    aff_ref[...] = e / jnp.sum(e, axis=0, keepdims=True)


def _outproj(x, mem, att, w, g, wr_t):
    n, d = x.shape
    half = mem.shape[1]
    tm = _tile(n, TILE_OUTPROJ)
    return pl.pallas_call(
        _outproj_kernel,
        grid=(n // tm,),
        in_specs=[pl.BlockSpec((tm, d), lambda i: (i, 0)),
                  pl.BlockSpec((tm, half), lambda i: (i, 0)),
                  pl.BlockSpec((tm, half), lambda i: (i, 0)),
                  pl.BlockSpec((2 * half, d), lambda i: (0, 0), pipeline_mode=pl.Buffered(1)),
                  pl.BlockSpec((1, d), lambda i: (0, 0)),
                  pl.BlockSpec((N_EXPERTS, d), lambda i: (0, 0))],
        out_specs=[pl.BlockSpec((tm, d), lambda i: (i, 0)),
                   pl.BlockSpec((tm, d), lambda i: (i, 0)),
                   pl.BlockSpec((N_EXPERTS, tm), lambda i: (0, i))],
        out_shape=[jax.ShapeDtypeStruct((n, d), F32),
                   jax.ShapeDtypeStruct((n, d), F32),
                   jax.ShapeDtypeStruct((N_EXPERTS, n), F32)],
        compiler_params=_params(("parallel",)),
        name="outproj_router",
    )(x, mem, att, w, g, wr_t)


def _row_copy(src_hbm, src_row, dst, dst_row, sem):
    return pltpu.make_async_copy(src_hbm.at[pl.ds(src_row, 1), :], dst.at[pl.ds(dst_row, 1), :], sem)


def _ffn_kernel(idx_ref, idx_next_ref, h_hbm, gate_ref, wg_ref, wu_ref, wd_ref, o_ref, xf_ref, xb_ref, sem,
                *, tm):
    nt, nf = pl.num_programs(1), pl.num_programs(2)
    f = pl.program_id(2)
    tile = pl.program_id(0) * nt + pl.program_id(1)
    slot = tile % 2
    rows_per_step = tm // nf

    def tile_wait(s):
        pltpu.make_async_copy(h_hbm.at[pl.ds(0, tm), :], xf_ref.at[s], sem.at[s]).wait()

    @pl.when(f == 0)
    def _():
        @pl.when(tile == 0)
        def _():
            def issue(r, carry):
                _row_copy(h_hbm, idx_ref[0, 0, r], xf_ref.at[0], r, sem.at[0]).start()
                return carry
            lax.fori_loop(0, tm, issue, 0)
        tile_wait(slot)
        xb_ref[...] = xf_ref[slot].astype(BF16)
        o_ref[...] = jnp.zeros_like(o_ref)

    for r in range(rows_per_step):
        row = f * rows_per_step + r
        _row_copy(h_hbm, idx_next_ref[0, 0, row], xf_ref.at[1 - slot], row, sem.at[1 - slot]).start()

    x = xb_ref[...]
    a = jnp.dot(x, wg_ref[...], preferred_element_type=F32)
    u = jnp.dot(x, wu_ref[...], preferred_element_type=F32)
    hid = (a * jax.nn.sigmoid(a) * u).astype(BF16)
    o_ref[...] += jnp.dot(hid, wd_ref[...], preferred_element_type=F32) * gate_ref[...]

    @pl.when((tile == pl.num_programs(0) * nt - 1) & (f == nf - 1))
    def _():
        tile_wait(1 - slot)


def _ffn(h, idx, gate, wg, wu, wd):
    n, d = h.shape
    e, cap = idx.shape
    ff = wg.shape[2]
    tm = _tile(cap, TILE_FFN_ROWS)
    tf = _tile(ff, TILE_FFN_HIDDEN)
    nt = cap // tm
    assert tm % (ff // tf) == 0
    idx3 = idx.reshape(e * nt, 1, tm)
    gate2 = gate.reshape(e * cap, 1)
    last = e * nt - 1
    return pl.pallas_call(
        functools.partial(_ffn_kernel, tm=tm),
        grid=(e, nt, ff // tf),
        in_specs=[pl.BlockSpec((1, 1, tm), lambda i, j, f: (i * nt + j, 0, 0), memory_space=pltpu.SMEM),
                  pl.BlockSpec((1, 1, tm), lambda i, j, f: (jnp.minimum(i * nt + j + 1, last), 0, 0),
                               memory_space=pltpu.SMEM),
                  pl.BlockSpec(memory_space=pl.ANY),
                  pl.BlockSpec((tm, 1), lambda i, j, f: (i * nt + j, 0)),
                  pl.BlockSpec((None, d, tf), lambda i, j, f: (i, 0, f)),
                  pl.BlockSpec((None, d, tf), lambda i, j, f: (i, 0, f)),
                  pl.BlockSpec((None, tf, d), lambda i, j, f: (i, f, 0))],
        out_specs=pl.BlockSpec((tm, d), lambda i, j, f: (i * nt + j, 0)),
        out_shape=jax.ShapeDtypeStruct((e * cap, d), F32),
        scratch_shapes=[pltpu.VMEM((2, tm, d), F32), pltpu.VMEM((tm, d), BF16), pltpu.SemaphoreType.DMA((2,))],
        compiler_params=_params(("arbitrary", "arbitrary", "arbitrary")),
        name="expert_ffn",
    )(idx3, idx3, h, gate2, wg, wu, wd)


WAIT_CHUNK = 256
COMBINE_COLS = 256
ISSUE_UNROLL = 8


def _combine_kernel(ent_ref, off_ref, jmax_ref, x_ref, ye_hbm, g_ref, o_ref, buf_ref, sem, *, tb, final_norm):
    t = pl.program_id(0)
    d = buf_ref.shape[2]
    slot = t % 2

    def fill(tile, s):
        for j in range(MAX_PLANES):
            @pl.when(j < jmax_ref[tile])
            def _():
                buf_ref[s, j * tb:(j + 1) * tb, :] = jnp.zeros((tb, d), F32)
        lo = off_ref[tile]
        n_ent = off_ref[tile + 1] - lo

        def issue(p):
            ent = ent_ref[p]
            _row_copy(ye_hbm, ent & 0xFFFF, buf_ref.at[s], ent >> 16, sem.at[s]).start()

        def issue_group(q, carry):
            for r in range(ISSUE_UNROLL):
                issue(lo + q * ISSUE_UNROLL + r)
            return carry
        groups = n_ent // ISSUE_UNROLL
        lax.fori_loop(0, groups, issue_group, 0)

        def issue_rest(p, carry):
            issue(p)
            return carry
        lax.fori_loop(lo + groups * ISSUE_UNROLL, lo + n_ent, issue_rest, 0)

    @pl.when(t == 0)
    def _():
        fill(0, 0)

    @pl.when(t + 1 < pl.num_programs(0))
    def _():
        fill(t + 1, 1 - slot)

    def wait_rows(k):
        pltpu.make_async_copy(ye_hbm.at[pl.ds(0, k), :], buf_ref.at[slot, pl.ds(0, k), :], sem.at[slot]).wait()

    count = off_ref[t + 1] - off_ref[t]

    def bulk(p, carry):
        wait_rows(WAIT_CHUNK)
        return carry
    lax.fori_loop(0, count // WAIT_CHUNK, bulk, 0)
    k = WAIT_CHUNK // 2
    while k >= 1:
        @pl.when((count & k) != 0)
        def _():
            wait_rows(k)
        k //= 2

    jmax = jmax_ref[t]
    for c0 in range(0, d, COMBINE_COLS):
        cols = slice(c0, min(c0 + COMBINE_COLS, d))

        def add(j, acc):
            return acc + buf_ref[slot, pl.ds(pl.multiple_of(j * tb, tb), tb), cols]
        o_ref[:, cols] = lax.fori_loop(0, jmax, add, x_ref[:, cols])

    if final_norm:
        o_ref[...] = _rms(o_ref[...], g_ref[...])


def _combine(x1, ye, entries, offsets, jmax, g, tb, final_norm):
    n, d = x1.shape
    assert ye.shape[0] >= WAIT_CHUNK and MAX_PLANES * tb >= WAIT_CHUNK
    grid_spec = pltpu.PrefetchScalarGridSpec(
        num_scalar_prefetch=3,
        grid=(n // tb,),
        in_specs=[pl.BlockSpec((tb, d), lambda i, *_: (i, 0)),
                  pl.BlockSpec(memory_space=pl.ANY),
                  pl.BlockSpec((1, d), lambda i, *_: (0, 0))],
        out_specs=pl.BlockSpec((tb, d), lambda i, *_: (i, 0)),
        scratch_shapes=[pltpu.VMEM((2, MAX_PLANES * tb, d), F32), pltpu.SemaphoreType.DMA((2,))],
    )
    return pl.pallas_call(
        functools.partial(_combine_kernel, tb=tb, final_norm=final_norm),
        grid_spec=grid_spec,
        out_shape=jax.ShapeDtypeStruct((n, d), F32),
        compiler_params=_params(("arbitrary",)),
        name="combine",
    )(entries, offsets, jmax, x1, ye, g)


def _route(aff_t, tb):
    e, n = aff_t.shape
    cap = CAPACITY_FACTOR * n // e
    gate, idx = lax.top_k(aff_t, cap)
    total = e * cap
    pos = jnp.arange(total, dtype=jnp.int32)
    tok, src = lax.sort_key_val(idx.reshape(-1), pos)
    new_tok = jnp.concatenate([jnp.ones((1,), bool), tok[1:] != tok[:-1]])
    plane = pos - lax.cummax(jnp.where(new_tok, pos, 0))
    entries = src | ((plane * tb + tok % tb) << 16)
    tiles = jnp.arange(n // tb + 1, dtype=jnp.int32)
    offsets = jnp.sum(tok[None, :] < (tiles * tb)[:, None], axis=1, dtype=jnp.int32)
    in_tile = (tok // tb)[None, :] == tiles[:-1, None]
    jmax = jnp.max(jnp.where(in_tile, plane[None, :] + 1, 0), axis=1)
    return gate, idx, entries, offsets, jmax


def _rope_tables(t):
    pos = jnp.arange(t)
    row = (pos // GRID_W).astype(F32)
    col = (pos % GRID_W).astype(F32)
    inv = ROPE_THETA ** (-jnp.arange(0, AXIS_DIM, 2, dtype=F32) / AXIS_DIM)
    ar, ac = row[:, None] * inv[None, :], col[:, None] * inv[None, :]
    cos = jnp.concatenate([jnp.cos(ar), jnp.cos(ar), jnp.cos(ac), jnp.cos(ac)], axis=-1)
    sin = jnp.concatenate([-jnp.sin(ar), jnp.sin(ar), -jnp.sin(ac), jnp.sin(ac)], axis=-1)
    return cos, sin


def _trunk(x, p):
    b, t, d = x.shape
    n = b * t
    assert N_EXPERTS * (CAPACITY_FACTOR * n // N_EXPERTS) <= 1 << 16
    cos, sin = _rope_tables(t)
    tk = _tile(t, TILE_ATT_KV)
    tq = _tile(t, TILE_ATT_Q)
    tb = _tile(n, TILE_COMBINE)
    xf = x.reshape(n, d)
    depth = p["w_in"].shape[0]
    for l in range(depth):
        proj, gates_t = _inproj(xf, p["norm1_g"][l], p["w_in"][l])
        proj = proj.reshape(b, t, IN_WIDTH_PAD)
        gates_t = jnp.swapaxes(gates_t.reshape(M_GATE_W, b, t), 0, 1)
        kr, vt = _kprep(proj, cos, sin, p["k_norm_g"][l], tk)
        att = _attention(proj, kr, vt, cos, sin, p["q_norm_g"][l], tq)
        qkm = _conv(proj, p["conv_w"][l], p["conv_b"][l])
        hfw = _mlstm(proj, qkm, gates_t, p["b_row"][l], p["b_col"][l], reverse=False)
        mem = _mlstm(proj, qkm, gates_t, p["b_row"][l], p["b_col"][l], reverse=True,
                     hfw=hfw, gn=p["mlstm_norm_g"][l])
        x1, h2, aff_t = _outproj(xf, mem.reshape(n, M_V_W), att.reshape(n, ATT_Q_W), p["w_out"][l],
                                 p["norm2_g"][l], p["w_router_t"][l])
        gate, idx, entries, offsets, jmax = _route(aff_t, tb)
        ye = _ffn(h2, idx, gate, p["w_gate"][l], p["w_up"][l], p["w_down"][l])
        xf = _combine(x1, ye, entries, offsets, jmax, p["final_norm_g"], tb,
                      final_norm=(l == depth - 1))
    return xf.reshape(b, t, d)


def _prepare(norm1_g, w_in, conv_w, conv_b, b_gates, q_norm_g, k_norm_g, mlstm_norm_g, w_out,
             norm2_g, w_router, w_gate, w_up, w_down, final_norm_g):
    depth, d, _ = w_in.shape
    return {
        "norm1_g": norm1_g.reshape(depth, 1, d),
        "w_in": jnp.pad(w_in.astype(BF16), ((0, 0), (0, 0), (0, IN_WIDTH_PAD - IN_WIDTH))),
        "conv_w": conv_w,
        "conv_b": conv_b.reshape(depth, 1, -1),
        "b_row": jnp.pad(b_gates, ((0, 0), (0, LANES - M_GATE_W))).reshape(depth, 1, LANES),
        "b_col": b_gates.reshape(depth, M_GATE_W, 1),
        "q_norm_g": q_norm_g.reshape(depth, 1, HEAD_DIM),
        "k_norm_g": k_norm_g.reshape(depth, 1, HEAD_DIM),
        "mlstm_norm_g": mlstm_norm_g.reshape(depth, 1, M_V_W),
        "w_out": w_out.astype(BF16),
        "norm2_g": norm2_g.reshape(depth, 1, d),
        "w_router_t": jnp.swapaxes(w_router, 1, 2),
        "w_gate": w_gate.astype(BF16),
        "w_up": w_up.astype(BF16),
        "w_down": w_down.astype(BF16),
        "final_norm_g": final_norm_g.reshape(1, d),
    }


def kernel(x_prompt, x_sample, norm1_g, w_in, conv_w, conv_b, b_gates, q_norm_g, k_norm_g, mlstm_norm_g,
           w_out, norm2_g, w_router, w_gate, w_up, w_down, final_norm_g):
    p = _prepare(norm1_g, w_in, conv_w, conv_b, b_gates, q_norm_g, k_norm_g, mlstm_norm_g, w_out,
                 norm2_g, w_router, w_gate, w_up, w_down, final_norm_g)
    return _trunk(x_prompt, p), _trunk(x_sample, p)
```

```python
import functools
import math

import jax
import jax.numpy as jnp
from jax import lax
from jax.experimental import pallas as pl
from jax.experimental.pallas import tpu as pltpu

F32 = jnp.float32
BF16 = jnp.bfloat16
EPS = 1e-6
LOG2E = 1.4426950408889634

LANES = 128
SUBLANES = 8
VMEM_BYTES_V7X = 64 * 1024 * 1024
VMEM_LIMIT = VMEM_BYTES_V7X - 8 * 1024 * 1024

GRID_W = 64
ATT_KV_HEADS = 2
ATT_GROUPS = 4
HEAD_DIM = 128
AXIS_DIM = HEAD_DIM // 2
ROPE_THETA = 10000.0
M_HEADS = 4
M_QK_DIM = 128
M_V_DIM = 256
M_CHUNK = 128
M_CONV = 5
N_GATE_KINDS = 4
N_EXPERTS = 16
CAPACITY_FACTOR = 2
ATT_Q_W = ATT_KV_HEADS * ATT_GROUPS * HEAD_DIM
ATT_KV_W = ATT_KV_HEADS * HEAD_DIM
M_QK_W = M_HEADS * M_QK_DIM
M_V_W = M_HEADS * M_V_DIM
M_GATE_W = N_GATE_KINDS * M_HEADS
COL_Q = 0
COL_K = COL_Q + ATT_Q_W
COL_V = COL_K + ATT_KV_W
COL_MQK = COL_V + ATT_KV_W
COL_MV = COL_MQK + 2 * M_QK_W
COL_MO = COL_MV + M_V_W
COL_MG = COL_MO + M_V_W
IN_WIDTH = COL_MG + M_GATE_W
IN_WIDTH_PAD = 4864
M_AUG = M_V_DIM + LANES
MAX_PLANES = N_EXPERTS

TILE_INPROJ = 256
TILE_ATT_Q = 256
TILE_ATT_KV = 1024
TILE_CONV = 512
TILE_OUTPROJ = 512
TILE_FFN_ROWS = 512
TILE_FFN_HIDDEN = 512
TILE_COMBINE = 128
MLSTM_SEQS_PER_STEP = 2


def _tile(dim, pref):
    t = min(dim, pref)
    assert dim % t == 0, (dim, pref)
    return t


def _params(sem, vmem=VMEM_LIMIT):
    return pltpu.CompilerParams(dimension_semantics=sem, vmem_limit_bytes=vmem)


def _rms(x, g):
    return x * lax.rsqrt(jnp.mean(x * x, axis=-1, keepdims=True) + EPS) * g


def _log_sigmoid(x):
    return jnp.minimum(x, 0.0) - jnp.log1p(jnp.exp(-jnp.abs(x)))


def _rope(x, cos, sin):
    lane = lax.broadcasted_iota(jnp.int32, x.shape, 1)
    first = (lane % AXIS_DIM) < (AXIS_DIM // 2)
    swapped = jnp.where(first, pltpu.roll(x, LANES - AXIS_DIM // 2, axis=1),
                        pltpu.roll(x, AXIS_DIM // 2, axis=1))
    return x * cos + swapped * sin


def _inproj_kernel(x_ref, g_ref, w_ref, o_ref, gt_ref, *, col_chunk):
    h = _rms(x_ref[...], g_ref[...]).astype(BF16)
    width = o_ref.shape[1]
    for c0 in range(0, width, col_chunk):
        c1 = min(c0 + col_chunk, width)
        o_ref[:, c0:c1] = jnp.dot(h, w_ref[:, c0:c1], preferred_element_type=F32)
    gt_ref[...] = o_ref[:, COL_MG:COL_MG + LANES].T[:M_GATE_W, :]


def _inproj(x, g, w, layer):
    n, d = x.shape
    p = w.shape[2]
    tm = _tile(n, TILE_INPROJ)
    return pl.pallas_call(
        functools.partial(_inproj_kernel, col_chunk=512),
        grid=(n // tm,),
        in_specs=[pl.BlockSpec((tm, d), lambda i: (i, 0)),
                  pl.BlockSpec((1, d), lambda i: (0, 0)),
                  pl.BlockSpec((None, d, p), lambda i: (layer, 0, 0), pipeline_mode=pl.Buffered(1))],
        out_specs=[pl.BlockSpec((tm, p), lambda i: (i, 0)),
                   pl.BlockSpec((M_GATE_W, tm), lambda i: (0, i))],
        out_shape=[jax.ShapeDtypeStruct((n, p), F32),
                   jax.ShapeDtypeStruct((M_GATE_W, n), F32)],
        compiler_params=_params(("parallel",)),
        name="inproj",
    )(x, g, w)


def _kprep_kernel(k_ref, v_ref, cos_ref, sin_ref, g_ref, ko_ref, vt_ref):
    cos, sin, g = cos_ref[...], sin_ref[...], g_ref[...]
    for h in range(ATT_KV_HEADS):
        k = _rms(k_ref[:, h * HEAD_DIM:(h + 1) * HEAD_DIM], g)
        ko_ref[h] = _rope(k, cos, sin).astype(BF16)
        vt_ref[h, 0] = v_ref[:, h * HEAD_DIM:(h + 1) * HEAD_DIM].T.astype(BF16)


def _kprep(proj, cos, sin, gk, tk):
    b, t, _ = proj.shape
    nkb = t // tk
    return pl.pallas_call(
        _kprep_kernel,
        grid=(b, nkb),
        in_specs=[pl.BlockSpec((None, tk, ATT_KV_W), lambda i, j: (i, j, COL_K // ATT_KV_W)),
                  pl.BlockSpec((None, tk, ATT_KV_W), lambda i, j: (i, j, COL_V // ATT_KV_W)),
                  pl.BlockSpec((tk, HEAD_DIM), lambda i, j: (j, 0)),
                  pl.BlockSpec((tk, HEAD_DIM), lambda i, j: (j, 0)),
                  pl.BlockSpec((1, HEAD_DIM), lambda i, j: (0, 0))],
        out_specs=[pl.BlockSpec((None, ATT_KV_HEADS, tk, HEAD_DIM), lambda i, j: (i, 0, j, 0)),
                   pl.BlockSpec((None, ATT_KV_HEADS, 1, HEAD_DIM, tk), lambda i, j: (i, 0, j, 0, 0))],
        out_shape=[jax.ShapeDtypeStruct((b, ATT_KV_HEADS, t, HEAD_DIM), BF16),
                   jax.ShapeDtypeStruct((b, ATT_KV_HEADS, nkb, HEAD_DIM, tk), BF16)],
        compiler_params=_params(("parallel", "parallel")),
        name="kprep",
    )(proj, proj, cos, sin, gk)


def _attn_kernel(q_ref, cos_ref, sin_ref, g_ref, k_ref, vt_ref, o_ref, s_ref, *, tq, tk, nkb):
    cos, sin, g = cos_ref[...], sin_ref[...], g_ref[...]
    qscale = HEAD_DIM ** -0.5 * LOG2E
    qs = []
    for i in range(ATT_GROUPS):
        q = _rms(q_ref[:, i * HEAD_DIM:(i + 1) * HEAD_DIM], g)
        qs.append((_rope(q, cos, sin) * qscale).astype(BF16))
    q4 = jnp.concatenate(qs, axis=0)
    nq = ATT_GROUPS * tq

    def scores(j):
        kb = k_ref[pl.ds(pl.multiple_of(j * tk, tk), tk), :]
        return lax.dot_general(kb, q4, (((1,), (1,)), ((), ())), preferred_element_type=F32)

    def softmax_step(j, slot, carry):
        m, l, acc = carry
        s = s_ref[slot]
        m_new = jnp.maximum(m, jnp.max(s, axis=0, keepdims=True))
        alpha = jnp.exp2(m - m_new)
        p = jnp.exp2(s - m_new)
        l = alpha * l + jnp.sum(p, axis=0, keepdims=True)
        acc = alpha * acc + jnp.dot(vt_ref[j], p.astype(BF16), preferred_element_type=F32)
        return m_new, l, acc

    def pair(jj, carry):
        j = 2 * jj
        s_ref[1] = scores(j + 1)
        carry = softmax_step(j, 0, carry)
        s_ref[0] = scores(j + 2)
        return softmax_step(j + 1, 1, carry)

    s_ref[0] = scores(0)
    carry = (jnp.full((1, nq), -jnp.inf, F32), jnp.zeros((1, nq), F32), jnp.zeros((HEAD_DIM, nq), F32))
    if nkb > 1:
        assert nkb % 2 == 0
        carry = lax.fori_loop(0, nkb // 2 - 1, pair, carry)
        s_ref[1] = scores(nkb - 1)
        carry = softmax_step(nkb - 2, 0, carry)
        carry = softmax_step(nkb - 1, 1, carry)
    else:
        carry = softmax_step(0, 0, carry)
    _, l, acc = carry
    o = acc / l
    for i in range(ATT_GROUPS):
        o_ref[:, i * HEAD_DIM:(i + 1) * HEAD_DIM] = o[:, i * tq:(i + 1) * tq].T.astype(BF16)


def _attention(proj, kr, vt, cos, sin, gq, tq):
    b, t, _ = proj.shape
    nkb, tk = vt.shape[2], vt.shape[4]
    gw = ATT_GROUPS * HEAD_DIM
    return pl.pallas_call(
        functools.partial(_attn_kernel, tq=tq, tk=tk, nkb=nkb),
        grid=(b, ATT_KV_HEADS, t // tq),
        in_specs=[pl.BlockSpec((None, tq, gw), lambda i, h, j: (i, j, h)),
                  pl.BlockSpec((tq, HEAD_DIM), lambda i, h, j: (j, 0)),
                  pl.BlockSpec((tq, HEAD_DIM), lambda i, h, j: (j, 0)),
                  pl.BlockSpec((1, HEAD_DIM), lambda i, h, j: (0, 0)),
                  pl.BlockSpec((None, None, t, HEAD_DIM), lambda i, h, j: (i, h, 0, 0)),
                  pl.BlockSpec((None, None, nkb, HEAD_DIM, tk), lambda i, h, j: (i, h, 0, 0, 0))],
        out_specs=pl.BlockSpec((None, tq, gw), lambda i, h, j: (i, j, h)),
        out_shape=jax.ShapeDtypeStruct((b, t, ATT_Q_W), BF16),
        scratch_shapes=[pltpu.VMEM((2, tk, ATT_GROUPS * tq), F32)],
        compiler_params=_params(("parallel", "parallel", "arbitrary")),
        name="attention",
    )(proj, cos, sin, gq, kr, vt)


def _conv_kernel(xq_ref, xk_ref, pq_ref, pk_ref, nq_ref, nk_ref, w_ref, b_ref, q_ref, kt_ref, pad_ref, *, tc):
    r = pl.program_id(1)
    nr = pl.num_programs(1)
    halo = SUBLANES
    for c, (x_ref, prev_ref, next_ref) in enumerate(((xq_ref, pq_ref, nq_ref), (xk_ref, pk_ref, nk_ref))):
        cols = slice(c * M_QK_W, (c + 1) * M_QK_W)
        pad_ref[0:halo, cols] = jnp.where(r > 0, prev_ref[...], 0.0)
        pad_ref[halo:halo + tc, cols] = x_ref[...]
        pad_ref[halo + tc:2 * halo + tc, cols] = jnp.where(r < nr - 1, next_ref[...], 0.0)
    acc = jnp.broadcast_to(b_ref[...], (tc, 2 * M_QK_W))
    for k in range(M_CONV):
        lo = halo - M_CONV // 2 + k
        acc = acc + w_ref[k:k + 1, :] * pad_ref[lo:lo + tc, :]
    y = acc * jax.nn.sigmoid(acc)
    q_ref[...] = y[:, :M_QK_W].astype(BF16)
    for hd in range(M_HEADS):
        kh = y[:, M_QK_W + hd * M_QK_DIM:M_QK_W + (hd + 1) * M_QK_DIM] * (M_QK_DIM ** -0.5)
        kt_ref[hd] = kh.T.astype(BF16)


def _conv(proj, w, bias):
    b, t, _ = proj.shape
    cw = M_QK_W
    tc = _tile(t, TILE_CONV)
    hb = tc // SUBLANES
    c0 = COL_MQK // cw
    assert c0 * cw == COL_MQK
    last = t // SUBLANES - 1

    def prev_map(c):
        return lambda i, r: (i, jnp.maximum(r * hb - 1, 0), c)

    def next_map(c):
        return lambda i, r: (i, jnp.minimum((r + 1) * hb, last), c)

    return pl.pallas_call(
        functools.partial(_conv_kernel, tc=tc),
        grid=(b, t // tc),
        in_specs=[pl.BlockSpec((None, tc, cw), lambda i, r: (i, r, c0)),
                  pl.BlockSpec((None, tc, cw), lambda i, r: (i, r, c0 + 1)),
                  pl.BlockSpec((None, SUBLANES, cw), prev_map(c0)),
                  pl.BlockSpec((None, SUBLANES, cw), prev_map(c0 + 1)),
                  pl.BlockSpec((None, SUBLANES, cw), next_map(c0)),
                  pl.BlockSpec((None, SUBLANES, cw), next_map(c0 + 1)),
                  pl.BlockSpec((M_CONV, 2 * cw), lambda i, r: (0, 0)),
                  pl.BlockSpec((1, 2 * cw), lambda i, r: (0, 0))],
        out_specs=[pl.BlockSpec((None, tc, M_QK_W), lambda i, r: (i, r, 0)),
                   pl.BlockSpec((None, M_HEADS, M_QK_DIM, tc), lambda i, r: (i, 0, 0, r))],
        out_shape=[jax.ShapeDtypeStruct((b, t, M_QK_W), BF16),
                   jax.ShapeDtypeStruct((b, M_HEADS, M_QK_DIM, t), BF16)],
        scratch_shapes=[pltpu.VMEM((tc + 2 * SUBLANES, 2 * cw), F32)],
        compiler_params=_params(("parallel", "parallel")),
        name="conv",
    )(proj, proj, proj, proj, proj, proj, w, bias)


def _mlstm_kernel(*refs, reverse, bb):
    if reverse:
        (q_all, kt_all, vlo_ref, vhi_ref, gcol_ref, grow_ref, bi_ref, bf_ref, bcol_ref,
         hfw_all, olo_ref, ohi_ref, gn_ref, out_all, c_ref, m_ref) = refs
    else:
        (q_all, kt_all, vlo_ref, vhi_ref, gcol_ref, grow_ref, bi_ref, bf_ref, bcol_ref,
         out_all, c_ref, m_ref) = refs
    L = M_CHUNK

    @pl.when(pl.program_id(1) == 0)
    def _():
        c_ref[...] = jnp.zeros_like(c_ref)
        m_ref[...] = jnp.zeros_like(m_ref)

    ii = lax.broadcasted_iota(jnp.int32, (L, L), 0)
    jj = lax.broadcasted_iota(jnp.int32, (L, L), 1)
    mask = (jj >= ii) if reverse else (jj <= ii)
    tri = mask.astype(F32)
    tri_t = ((ii >= jj) if reverse else (ii <= jj)).astype(F32)
    kind = 2 * M_HEADS if reverse else 0
    lane = lax.broadcasted_iota(jnp.int32, (L, LANES), 1)
    row = lax.broadcasted_iota(jnp.int32, (L, LANES), 0)
    last_row = 0 if reverse else L - 1

    def running_max(a):
        s = 1
        while s < L:
            if reverse:
                shifted = jnp.where(row < L - s, pltpu.roll(a, L - s, axis=0), -jnp.inf)
            else:
                shifted = jnp.where(row >= s, pltpu.roll(a, s, axis=0), -jnp.inf)
            a = jnp.maximum(a, shifted)
            s *= 2
        return a

    m_all = m_ref[...]
    m_out = []
    for bi in range(bb):
        g_in = gcol_ref[bi, :, 0:LANES] + bi_ref[...]
        logf = _log_sigmoid(gcol_ref[bi, :, LANES:2 * LANES] + bf_ref[...])
        cum = jnp.dot(tri, logf, preferred_element_type=F32, precision=lax.Precision.HIGHEST)
        tot = jnp.sum(logf, axis=0, keepdims=True)
        m_prev = m_all[bi:bi + 1, :]
        u = jnp.maximum(running_max(g_in - cum), m_prev)
        u_last = u[last_row:last_row + 1, :]
        e_neg_m = jnp.exp(-(cum + u))
        s_prev_row = jnp.exp(m_prev - u_last)
        m_out.append(tot + u_last)
        gr = grow_ref[bi] + bcol_ref[...]
        cum_r = jnp.dot(_log_sigmoid(gr), tri_t, preferred_element_type=F32, precision=lax.Precision.HIGHEST)

        q_ref, out_ref = q_all.at[bi], out_all.at[bi]
        for hd in range(M_HEADS):
            st = bi * M_HEADS + hd
            li = kind + hd
            a_row = gr[li:li + 1, :] - cum_r[li + M_HEADS:li + M_HEADS + 1, :]
            u_b = jnp.broadcast_to(u[:, li:li + 1], (L, LANES))
            w_intra = jnp.where(mask, jnp.exp(a_row - u_b), 0.0)
            w_inter = jnp.exp(m_prev[:, li:li + 1] - u_b)
            q = q_ref[:, hd * M_QK_DIM:(hd + 1) * M_QK_DIM]
            kt = kt_all[bi, hd]
            v_ref = (vlo_ref if hd < M_HEADS // 2 else vhi_ref).at[bi]
            vo = (hd % (M_HEADS // 2)) * M_V_DIM
            ones_col = jnp.where(lane == li, 1.0, 0.0).astype(BF16)
            v_aug = jnp.concatenate([v_ref[:, vo:vo + M_V_DIM].astype(BF16), ones_col], axis=1)

            s = jnp.dot(q, kt, preferred_element_type=F32) * w_intra
            c_aug = c_ref[st]
            lhs = jnp.concatenate([s.astype(BF16), (q.astype(F32) * w_inter).astype(BF16)], axis=1)
            rhs = jnp.concatenate([v_aug, c_aug.astype(BF16)], axis=0)
            num = jnp.dot(lhs, rhs, preferred_element_type=F32)
            inv = 1.0 / jnp.maximum(jnp.abs(num[:, M_V_DIM:]), e_neg_m)
            h = num[:, :M_V_DIM] * inv[:, li:li + 1]

            w_g = jnp.exp(a_row - u_last[:, li:li + 1])
            kw = (kt.astype(F32) * w_g).astype(BF16)
            c_ref[st] = s_prev_row[:, li:li + 1] * c_aug + jnp.dot(kw, v_aug, preferred_element_type=F32)

            cs = slice(hd * M_V_DIM, (hd + 1) * M_V_DIM)
            if reverse:
                o_ref = (olo_ref if hd < M_HEADS // 2 else ohi_ref).at[bi]
                hs = hfw_all[bi, :, cs] + h
                y = _rms(hs, gn_ref[:, cs]) * jax.nn.sigmoid(o_ref[:, vo:vo + M_V_DIM])
                out_ref[:, cs] = y.astype(BF16)
            else:
                out_ref[:, cs] = h
    pad_rows = m_ref.shape[0] - bb
    m_ref[...] = jnp.concatenate(m_out + ([jnp.zeros((pad_rows, LANES), F32)] if pad_rows else []), axis=0)


def _mlstm(proj, qm, kt, gates_t, b_in, b_f, b_col, reverse, hfw=None, gn=None):
    b, t, _ = proj.shape
    nc = t // M_CHUNK
    half = M_V_W // 2
    bb = _tile(b, MLSTM_SEQS_PER_STEP)
    n_state = bb * M_HEADS
    assert bb <= SUBLANES and COL_MG % (2 * LANES) == 0

    def cidx(c):
        return nc - 1 - c if reverse else c

    in_specs = [pl.BlockSpec((bb, M_CHUNK, M_QK_W), lambda i, c: (i, cidx(c), 0)),
                pl.BlockSpec((bb, M_HEADS, M_QK_DIM, M_CHUNK), lambda i, c: (i, 0, 0, cidx(c))),
                pl.BlockSpec((bb, M_CHUNK, half), lambda i, c: (i, cidx(c), COL_MV // half)),
                pl.BlockSpec((bb, M_CHUNK, half), lambda i, c: (i, cidx(c), COL_MV // half + 1)),
                pl.BlockSpec((bb, M_CHUNK, 2 * LANES), lambda i, c: (i, cidx(c), COL_MG // (2 * LANES))),
                pl.BlockSpec((bb, M_GATE_W, M_CHUNK), lambda i, c: (i, 0, cidx(c))),
                pl.BlockSpec((1, LANES), lambda i, c: (0, 0)),
                pl.BlockSpec((1, LANES), lambda i, c: (0, 0)),
                pl.BlockSpec((M_GATE_W, 1), lambda i, c: (0, 0))]
    args = [qm, kt, proj, proj, proj, gates_t, b_in, b_f, b_col]
    if reverse:
        in_specs += [pl.BlockSpec((bb, M_CHUNK, M_V_W), lambda i, c: (i, cidx(c), 0)),
                     pl.BlockSpec((bb, M_CHUNK, half), lambda i, c: (i, cidx(c), COL_MO // half)),
                     pl.BlockSpec((bb, M_CHUNK, half), lambda i, c: (i, cidx(c), COL_MO // half + 1)),
                     pl.BlockSpec((1, M_V_W), lambda i, c: (0, 0))]
        args += [hfw, proj, proj, gn]
    return pl.pallas_call(
        functools.partial(_mlstm_kernel, reverse=reverse, bb=bb),
        grid=(b // bb, nc),
        in_specs=in_specs,
        out_specs=pl.BlockSpec((bb, M_CHUNK, M_V_W), lambda i, c: (i, cidx(c), 0)),
        out_shape=jax.ShapeDtypeStruct((b, t, M_V_W), BF16 if reverse else F32),
        scratch_shapes=[pltpu.VMEM((n_state, M_QK_DIM, M_AUG), F32),
                        pltpu.VMEM((SUBLANES, LANES), F32)],
        compiler_params=_params(("parallel", "arbitrary")),
        name="mlstm_bw" if reverse else "mlstm_fw",
    )(*args)


def _outproj_kernel(x_ref, mem_ref, att_ref, w_ref, g_ref, wr_ref, x1_ref, h_ref, aff_ref):
    half = mem_ref.shape[1]
    y = (jnp.dot(mem_ref[...], w_ref[0:half, :], preferred_element_type=F32)
         + jnp.dot(att_ref[...], w_ref[half:2 * half, :], preferred_element_type=F32))
    x1 = x_ref[...] + y
    x1_ref[...] = x1
    h = _rms(x1, g_ref[...])
    h_ref[...] = h
    logits = lax.dot_general(wr_ref[...], h, (((1,), (1,)), ((), ())), preferred_element_type=F32,
                             precision=lax.Precision.HIGHEST)
    e = jnp.exp(logits - jnp.max(logits, axis=0, keepdims=True))
    aff_ref[...] = e / jnp.sum(e, axis=0, keepdims=True)


def _outproj(x, mem, att, w, layer, g, wr_t):
    n, d = x.shape
    half = mem.shape[1]
    tm = _tile(n, TILE_OUTPROJ)
    return pl.pallas_call(
        _outproj_kernel,
        grid=(n // tm,),
        in_specs=[pl.BlockSpec((tm, d), lambda i: (i, 0)),
                  pl.BlockSpec((tm, half), lambda i: (i, 0)),
                  pl.BlockSpec((tm, half), lambda i: (i, 0)),
                  pl.BlockSpec((None, 2 * half, d), lambda i: (layer, 0, 0), pipeline_mode=pl.Buffered(1)),
                  pl.BlockSpec((1, d), lambda i: (0, 0)),
                  pl.BlockSpec((N_EXPERTS, d), lambda i: (0, 0))],
        out_specs=[pl.BlockSpec((tm, d), lambda i: (i, 0)),
                   pl.BlockSpec((tm, d), lambda i: (i, 0)),
                   pl.BlockSpec((N_EXPERTS, tm), lambda i: (0, i))],
        out_shape=[jax.ShapeDtypeStruct((n, d), F32),
                   jax.ShapeDtypeStruct((n, d), F32),
                   jax.ShapeDtypeStruct((N_EXPERTS, n), F32)],
        compiler_params=_params(("parallel",)),
        name="outproj_router",
    )(x, mem, att, w, g, wr_t)


def _row_copy(src_hbm, src_row, dst, dst_row, sem):
    return pltpu.make_async_copy(src_hbm.at[pl.ds(src_row, 1), :], dst.at[pl.ds(dst_row, 1), :], sem)


def _ffn_kernel(idx_ref, idx_next_ref, h_hbm, gate_ref, wg_ref, wu_ref, wd_ref, o_ref, xf_ref, xb_ref, sem,
                *, tm):
    nt, nf = pl.num_programs(1), pl.num_programs(2)
    f = pl.program_id(2)
    tile = pl.program_id(0) * nt + pl.program_id(1)
    slot = tile % 2
    rows_per_step = tm // nf

    def tile_wait(s):
        pltpu.make_async_copy(h_hbm.at[pl.ds(0, tm), :], xf_ref.at[s], sem.at[s]).wait()

    @pl.when(f == 0)
    def _():
        @pl.when(tile == 0)
        def _():
            def issue(r, carry):
                _row_copy(h_hbm, idx_ref[0, 0, r], xf_ref.at[0], r, sem.at[0]).start()
                return carry
            lax.fori_loop(0, tm, issue, 0)
        tile_wait(slot)
        xb_ref[...] = xf_ref[slot].astype(BF16)
        o_ref[...] = jnp.zeros_like(o_ref)

    for r in range(rows_per_step):
        row = f * rows_per_step + r
        _row_copy(h_hbm, idx_next_ref[0, 0, row], xf_ref.at[1 - slot], row, sem.at[1 - slot]).start()

    x = xb_ref[...]
    a = jnp.dot(x, wg_ref[...], preferred_element_type=F32)
    u = jnp.dot(x, wu_ref[...], preferred_element_type=F32)
    hid = (a * jax.nn.sigmoid(a) * u).astype(BF16)
    o_ref[...] += jnp.dot(hid, wd_ref[...], preferred_element_type=F32) * gate_ref[...]

    @pl.when((tile == pl.num_programs(0) * nt - 1) & (f == nf - 1))
    def _():
        tile_wait(1 - slot)


def _ffn(h, idx, gate, wg, wu, wd, layer):
    n, d = h.shape
    e, cap = idx.shape
    ff = wg.shape[3]
    tm = _tile(cap, TILE_FFN_ROWS)
    tf = _tile(ff, TILE_FFN_HIDDEN)
    nt = cap // tm
    assert tm % (ff // tf) == 0
    idx3 = idx.reshape(e * nt, 1, tm)
    gate2 = gate.reshape(e * cap, 1)
    last = e * nt - 1
    return pl.pallas_call(
        functools.partial(_ffn_kernel, tm=tm),
        grid=(e, nt, ff // tf),
        in_specs=[pl.BlockSpec((1, 1, tm), lambda i, j, f: (i * nt + j, 0, 0), memory_space=pltpu.SMEM),
                  pl.BlockSpec((1, 1, tm), lambda i, j, f: (jnp.minimum(i * nt + j + 1, last), 0, 0),
                               memory_space=pltpu.SMEM),
                  pl.BlockSpec(memory_space=pl.ANY),
                  pl.BlockSpec((tm, 1), lambda i, j, f: (i * nt + j, 0)),
                  pl.BlockSpec((None, None, d, tf), lambda i, j, f: (layer, i, 0, f)),
                  pl.BlockSpec((None, None, d, tf), lambda i, j, f: (layer, i, 0, f)),
                  pl.BlockSpec((None, None, tf, d), lambda i, j, f: (layer, i, f, 0))],
        out_specs=pl.BlockSpec((tm, d), lambda i, j, f: (i * nt + j, 0)),
        out_shape=jax.ShapeDtypeStruct((e * cap, d), F32),
        scratch_shapes=[pltpu.VMEM((2, tm, d), F32), pltpu.VMEM((tm, d), BF16), pltpu.SemaphoreType.DMA((2,))],
        compiler_params=_params(("arbitrary", "arbitrary", "arbitrary")),
        name="expert_ffn",
    )(idx3, idx3, h, gate2, wg, wu, wd)


WAIT_CHUNK = 256
COMBINE_COLS = 256
ISSUE_UNROLL = 8


def _combine_kernel(ent_ref, off_ref, jmax_ref, x_ref, ye_hbm, g_ref, o_ref, buf_ref, sem, *, tb, final_norm):
    t = pl.program_id(0)
    d = buf_ref.shape[2]
    slot = t % 2

    def fill(tile, s):
        for j in range(MAX_PLANES):
            @pl.when(j < jmax_ref[tile])
            def _():
                buf_ref[s, j * tb:(j + 1) * tb, :] = jnp.zeros((tb, d), F32)
        lo = off_ref[tile]
        n_ent = off_ref[tile + 1] - lo

        def issue(p):
            ent = ent_ref[p]
            _row_copy(ye_hbm, ent & 0xFFFF, buf_ref.at[s], ent >> 16, sem.at[s]).start()

        def issue_group(q, carry):
            for r in range(ISSUE_UNROLL):
                issue(lo + q * ISSUE_UNROLL + r)
            return carry
        groups = n_ent // ISSUE_UNROLL
        lax.fori_loop(0, groups, issue_group, 0)

        def issue_rest(p, carry):
            issue(p)
            return carry
        lax.fori_loop(lo + groups * ISSUE_UNROLL, lo + n_ent, issue_rest, 0)

    @pl.when(t == 0)
    def _():
        fill(0, 0)

    @pl.when(t + 1 < pl.num_programs(0))
    def _():
        fill(t + 1, 1 - slot)

    def wait_rows(k):
        pltpu.make_async_copy(ye_hbm.at[pl.ds(0, k), :], buf_ref.at[slot, pl.ds(0, k), :], sem.at[slot]).wait()

    count = off_ref[t + 1] - off_ref[t]

    def bulk(p, carry):
        wait_rows(WAIT_CHUNK)
        return carry
    lax.fori_loop(0, count // WAIT_CHUNK, bulk, 0)
    k = WAIT_CHUNK // 2
    while k >= 1:
        @pl.when((count & k) != 0)
        def _():
            wait_rows(k)
        k //= 2

    jmax = jmax_ref[t]
    for c0 in range(0, d, COMBINE_COLS):
        cols = slice(c0, min(c0 + COMBINE_COLS, d))

        def add(j, acc):
            return acc + buf_ref[slot, pl.ds(pl.multiple_of(j * tb, tb), tb), cols]
        o_ref[:, cols] = lax.fori_loop(0, jmax, add, x_ref[:, cols])

    if final_norm:
        o_ref[...] = _rms(o_ref[...], g_ref[...])


def _combine(x1, ye, entries, offsets, jmax, g, tb, final_norm):
    n, d = x1.shape
    assert ye.shape[0] >= WAIT_CHUNK and MAX_PLANES * tb >= WAIT_CHUNK
    grid_spec = pltpu.PrefetchScalarGridSpec(
        num_scalar_prefetch=3,
        grid=(n // tb,),
        in_specs=[pl.BlockSpec((tb, d), lambda i, *_: (i, 0)),
                  pl.BlockSpec(memory_space=pl.ANY),
                  pl.BlockSpec((1, d), lambda i, *_: (0, 0))],
        out_specs=pl.BlockSpec((tb, d), lambda i, *_: (i, 0)),
        scratch_shapes=[pltpu.VMEM((2, MAX_PLANES * tb, d), F32), pltpu.SemaphoreType.DMA((2,))],
    )
    return pl.pallas_call(
        functools.partial(_combine_kernel, tb=tb, final_norm=final_norm),
        grid_spec=grid_spec,
        out_shape=jax.ShapeDtypeStruct((n, d), F32),
        compiler_params=_params(("arbitrary",)),
        name="combine",
    )(entries, offsets, jmax, x1, ye, g)


def _route(aff_t, tb):
    e, n = aff_t.shape
    cap = CAPACITY_FACTOR * n // e
    gate, idx = lax.top_k(aff_t, cap)
    total = e * cap
    pos = jnp.arange(total, dtype=jnp.int32)
    tok, src = lax.sort_key_val(idx.reshape(-1), pos)
    new_tok = jnp.concatenate([jnp.ones((1,), bool), tok[1:] != tok[:-1]])
    plane = pos - lax.cummax(jnp.where(new_tok, pos, 0))
    entries = src | ((plane * tb + tok % tb) << 16)
    tiles = jnp.arange(n // tb + 1, dtype=jnp.int32)
    offsets = jnp.sum(tok[None, :] < (tiles * tb)[:, None], axis=1, dtype=jnp.int32)
    in_tile = (tok // tb)[None, :] == tiles[:-1, None]
    jmax = jnp.max(jnp.where(in_tile, plane[None, :] + 1, 0), axis=1)
    return gate, idx, entries, offsets, jmax


def _rope_tables(t):
    pos = jnp.arange(t)
    row = (pos // GRID_W).astype(F32)
    col = (pos % GRID_W).astype(F32)
    inv = ROPE_THETA ** (-jnp.arange(0, AXIS_DIM, 2, dtype=F32) / AXIS_DIM)
    ar, ac = row[:, None] * inv[None, :], col[:, None] * inv[None, :]
    cos = jnp.concatenate([jnp.cos(ar), jnp.cos(ar), jnp.cos(ac), jnp.cos(ac)], axis=-1)
    sin = jnp.concatenate([-jnp.sin(ar), jnp.sin(ar), -jnp.sin(ac), jnp.sin(ac)], axis=-1)
    return cos, sin


def _trunk(x, p):
    b, t, d = x.shape
    n = b * t
    assert N_EXPERTS * (CAPACITY_FACTOR * n // N_EXPERTS) <= 1 << 16
    cos, sin = _rope_tables(t)
    tk = _tile(t, TILE_ATT_KV)
    tq = _tile(t, TILE_ATT_Q)
    tb = _tile(n, TILE_COMBINE)
    xf = x.reshape(n, d)
    depth = p["w_in"].shape[0]
    for l in range(depth):
        proj, gates_t = _inproj(xf, p["norm1_g"][l], p["w_in"], l)
        proj = proj.reshape(b, t, IN_WIDTH_PAD)
        gates_t = jnp.swapaxes(gates_t.reshape(M_GATE_W, b, t), 0, 1)
        kr, vt = _kprep(proj, cos, sin, p["k_norm_g"][l], tk)
        att = _attention(proj, kr, vt, cos, sin, p["q_norm_g"][l], tq)
        qm, kt = _conv(proj, p["conv_w"][l], p["conv_b"][l])
        gate_bias = (p["b_in"][l], p["b_f"][l], p["b_col"][l])
        hfw = _mlstm(proj, qm, kt, gates_t, *gate_bias, reverse=False)
        mem = _mlstm(proj, qm, kt, gates_t, *gate_bias, reverse=True, hfw=hfw, gn=p["mlstm_norm_g"][l])
        x1, h2, aff_t = _outproj(xf, mem.reshape(n, M_V_W), att.reshape(n, ATT_Q_W), p["w_out"], l,
                                 p["norm2_g"][l], p["w_router_t"][l])
        gate, idx, entries, offsets, jmax = _route(aff_t, tb)
        ye = _ffn(h2, idx, gate, p["w_gate"], p["w_up"], p["w_down"], l)
        xf = _combine(x1, ye, entries, offsets, jmax, p["final_norm_g"], tb,
                      final_norm=(l == depth - 1))
    return xf.reshape(b, t, d)


def _prepare(norm1_g, w_in, conv_w, conv_b, b_gates, q_norm_g, k_norm_g, mlstm_norm_g, w_out,
             norm2_g, w_router, w_gate, w_up, w_down, final_norm_g):
    depth, d, _ = w_in.shape

    def forget_block(gates):
        pieces = []
        for kind in range(0, N_GATE_KINDS, 2):
            f_lo = (kind + 1) * M_HEADS
            pieces += [gates[..., f_lo:f_lo + M_HEADS], jnp.zeros(gates.shape[:-1] + (M_HEADS,), gates.dtype)]
        pieces.append(jnp.zeros(gates.shape[:-1] + (LANES - M_GATE_W,), gates.dtype))
        return jnp.concatenate(pieces, axis=-1)

    w_in_b = w_in.astype(BF16)
    first_block_pad = jnp.zeros((depth, d, COL_MG + LANES - IN_WIDTH), BF16)
    assert COL_MG + 2 * LANES == IN_WIDTH_PAD
    return {
        "norm1_g": norm1_g.reshape(depth, 1, d),
        "w_in": jnp.concatenate([w_in_b, first_block_pad, forget_block(w_in_b[..., COL_MG:])], axis=-1),
        "conv_w": conv_w,
        "conv_b": conv_b.reshape(depth, 1, -1),
        "b_in": jnp.pad(b_gates, ((0, 0), (0, LANES - M_GATE_W))).reshape(depth, 1, LANES),
        "b_f": forget_block(b_gates).reshape(depth, 1, LANES),
        "b_col": b_gates.reshape(depth, M_GATE_W, 1),
        "q_norm_g": q_norm_g.reshape(depth, 1, HEAD_DIM),
        "k_norm_g": k_norm_g.reshape(depth, 1, HEAD_DIM),
        "mlstm_norm_g": mlstm_norm_g.reshape(depth, 1, M_V_W),
        "w_out": w_out.astype(BF16),
        "norm2_g": norm2_g.reshape(depth, 1, d),
        "w_router_t": jnp.swapaxes(w_router, 1, 2),
        "w_gate": w_gate.astype(BF16),
        "w_up": w_up.astype(BF16),
        "w_down": w_down.astype(BF16),
        "final_norm_g": final_norm_g.reshape(1, d),
    }


def kernel(x_prompt, x_sample, norm1_g, w_in, conv_w, conv_b, b_gates, q_norm_g, k_norm_g, mlstm_norm_g,
           w_out, norm2_g, w_router, w_gate, w_up, w_down, final_norm_g):
    p = _prepare(norm1_g, w_in, conv_w, conv_b, b_gates, q_norm_g, k_norm_g, mlstm_norm_g, w_out,
                 norm2_g, w_router, w_gate, w_up, w_down, final_norm_g)
    return _trunk(x_prompt, p), _trunk(x_sample, p)
```

```python
import functools
import math

import jax
import jax.numpy as jnp
from jax import lax
from jax.experimental import pallas as pl
from jax.experimental.pallas import tpu as pltpu

F32 = jnp.float32
BF16 = jnp.bfloat16
EPS = 1e-6
LOG2E = 1.4426950408889634

LANES = 128
SUBLANES = 8
VMEM_BYTES_V7X = 64 * 1024 * 1024
VMEM_LIMIT = VMEM_BYTES_V7X - 8 * 1024 * 1024

GRID_W = 64
ATT_KV_HEADS = 2
ATT_GROUPS = 4
HEAD_DIM = 128
AXIS_DIM = HEAD_DIM // 2
ROPE_THETA = 10000.0
M_HEADS = 4
M_QK_DIM = 128
M_V_DIM = 256
M_CHUNK = 128
M_CONV = 5
N_GATE_KINDS = 4
N_EXPERTS = 16
CAPACITY_FACTOR = 2
ATT_Q_W = ATT_KV_HEADS * ATT_GROUPS * HEAD_DIM
ATT_KV_W = ATT_KV_HEADS * HEAD_DIM
M_QK_W = M_HEADS * M_QK_DIM
M_V_W = M_HEADS * M_V_DIM
M_GATE_W = N_GATE_KINDS * M_HEADS
COL_Q = 0
COL_K = COL_Q + ATT_Q_W
COL_V = COL_K + ATT_KV_W
COL_MQK = COL_V + ATT_KV_W
COL_MV = COL_MQK + 2 * M_QK_W
COL_MO = COL_MV + M_V_W
COL_MG = COL_MO + M_V_W
IN_WIDTH = COL_MG + M_GATE_W
IN_WIDTH_PAD = 4864
M_AUG = M_V_DIM + LANES
MAX_PLANES = N_EXPERTS

TILE_INPROJ = 256
TILE_ATT_Q = 256
TILE_ATT_KV = 1024
ATT_MIN_KV_BLOCKS = 4
TILE_CONV = 512
TILE_OUTPROJ = 512
TILE_FFN_ROWS = 512
TILE_FFN_HIDDEN = 512
TILE_COMBINE = 128
MLSTM_SEQS_PER_STEP = 2


def _tile(dim, pref):
    t = min(dim, pref)
    assert dim % t == 0, (dim, pref)
    return t


def _params(sem, vmem=VMEM_LIMIT):
    return pltpu.CompilerParams(dimension_semantics=sem, vmem_limit_bytes=vmem)


def _rms(x, g):
    return x * lax.rsqrt(jnp.mean(x * x, axis=-1, keepdims=True) + EPS) * g


def _log_sigmoid(x):
    return jnp.minimum(x, 0.0) - jnp.log1p(jnp.exp(-jnp.abs(x)))


def _rope(x, cos, sin):
    lane = lax.broadcasted_iota(jnp.int32, x.shape, 1)
    first = (lane % AXIS_DIM) < (AXIS_DIM // 2)
    swapped = jnp.where(first, pltpu.roll(x, LANES - AXIS_DIM // 2, axis=1),
                        pltpu.roll(x, AXIS_DIM // 2, axis=1))
    return x * cos + swapped * sin


def _inproj_kernel(x_ref, g_ref, w_ref, o_ref, gt_ref, *, col_chunk):
    h = _rms(x_ref[...], g_ref[...]).astype(BF16)
    width = o_ref.shape[1]
    for c0 in range(0, width, col_chunk):
        c1 = min(c0 + col_chunk, width)
        o_ref[:, c0:c1] = jnp.dot(h, w_ref[:, c0:c1], preferred_element_type=F32)
    gt_ref[...] = o_ref[:, COL_MG:COL_MG + LANES].T[:M_GATE_W, :]


def _inproj(x, g, w, layer):
    n, d = x.shape
    p = w.shape[2]
    tm = _tile(n, TILE_INPROJ)
    return pl.pallas_call(
        functools.partial(_inproj_kernel, col_chunk=512),
        grid=(n // tm,),
        in_specs=[pl.BlockSpec((tm, d), lambda i: (i, 0)),
                  pl.BlockSpec((1, d), lambda i: (0, 0)),
                  pl.BlockSpec((None, d, p), lambda i: (layer, 0, 0), pipeline_mode=pl.Buffered(1))],
        out_specs=[pl.BlockSpec((tm, p), lambda i: (i, 0)),
                   pl.BlockSpec((M_GATE_W, tm), lambda i: (0, i))],
        out_shape=[jax.ShapeDtypeStruct((n, p), F32),
                   jax.ShapeDtypeStruct((M_GATE_W, n), F32)],
        compiler_params=_params(("parallel",)),
        name="inproj",
    )(x, g, w)


def _kprep_kernel(k_ref, v_ref, cos_ref, sin_ref, g_ref, ko_ref, vt_ref):
    cos, sin, g = cos_ref[...], sin_ref[...], g_ref[...]
    for h in range(ATT_KV_HEADS):
        k = _rms(k_ref[:, h * HEAD_DIM:(h + 1) * HEAD_DIM], g)
        ko_ref[h] = _rope(k, cos, sin).astype(BF16)
        vt_ref[h, 0] = v_ref[:, h * HEAD_DIM:(h + 1) * HEAD_DIM].T.astype(BF16)


def _kprep(proj, cos, sin, gk, tk):
    b, t, _ = proj.shape
    nkb = t // tk
    return pl.pallas_call(
        _kprep_kernel,
        grid=(b, nkb),
        in_specs=[pl.BlockSpec((None, tk, ATT_KV_W), lambda i, j: (i, j, COL_K // ATT_KV_W)),
                  pl.BlockSpec((None, tk, ATT_KV_W), lambda i, j: (i, j, COL_V // ATT_KV_W)),
                  pl.BlockSpec((tk, HEAD_DIM), lambda i, j: (j, 0)),
                  pl.BlockSpec((tk, HEAD_DIM), lambda i, j: (j, 0)),
                  pl.BlockSpec((1, HEAD_DIM), lambda i, j: (0, 0))],
        out_specs=[pl.BlockSpec((None, ATT_KV_HEADS, tk, HEAD_DIM), lambda i, j: (i, 0, j, 0)),
                   pl.BlockSpec((None, ATT_KV_HEADS, 1, HEAD_DIM, tk), lambda i, j: (i, 0, j, 0, 0))],
        out_shape=[jax.ShapeDtypeStruct((b, ATT_KV_HEADS, t, HEAD_DIM), BF16),
                   jax.ShapeDtypeStruct((b, ATT_KV_HEADS, nkb, HEAD_DIM, tk), BF16)],
        compiler_params=_params(("parallel", "parallel")),
        name="kprep",
    )(proj, proj, cos, sin, gk)


def _attn_kernel(q_ref, cos_ref, sin_ref, g_ref, k_ref, vt_ref, o_ref, s_ref, *, tq, tk, nkb):
    cos, sin, g = cos_ref[...], sin_ref[...], g_ref[...]
    qscale = HEAD_DIM ** -0.5 * LOG2E
    qs = []
    for i in range(ATT_GROUPS):
        q = _rms(q_ref[:, i * HEAD_DIM:(i + 1) * HEAD_DIM], g)
        qs.append((_rope(q, cos, sin) * qscale).astype(BF16))
    q4 = jnp.concatenate(qs, axis=0)
    nq = ATT_GROUPS * tq

    def scores(j):
        kb = k_ref[pl.ds(pl.multiple_of(j * tk, tk), tk), :]
        return lax.dot_general(kb, q4, (((1,), (1,)), ((), ())), preferred_element_type=F32)

    def softmax_step(j, slot, carry):
        m, l, acc = carry
        s = s_ref[slot]
        m_new = jnp.maximum(m, jnp.max(s, axis=0, keepdims=True))
        alpha = jnp.exp2(m - m_new)
        p = jnp.exp2(s - m_new)
        l = alpha * l + jnp.sum(p, axis=0, keepdims=True)
        acc = alpha * acc + jnp.dot(vt_ref[j], p.astype(BF16), preferred_element_type=F32)
        return m_new, l, acc

    def pair(jj, carry):
        j = 2 * jj
        s_ref[1] = scores(j + 1)
        carry = softmax_step(j, 0, carry)
        s_ref[0] = scores(j + 2)
        return softmax_step(j + 1, 1, carry)

    s_ref[0] = scores(0)
    carry = (jnp.full((1, nq), -jnp.inf, F32), jnp.zeros((1, nq), F32), jnp.zeros((HEAD_DIM, nq), F32))
    if nkb > 1:
        assert nkb % 2 == 0
        carry = lax.fori_loop(0, nkb // 2 - 1, pair, carry)
        s_ref[1] = scores(nkb - 1)
        carry = softmax_step(nkb - 2, 0, carry)
        carry = softmax_step(nkb - 1, 1, carry)
    else:
        carry = softmax_step(0, 0, carry)
    _, l, acc = carry
    o = acc / l
    for i in range(ATT_GROUPS):
        o_ref[:, i * HEAD_DIM:(i + 1) * HEAD_DIM] = o[:, i * tq:(i + 1) * tq].T.astype(BF16)


def _attention(proj, kr, vt, cos, sin, gq, tq):
    b, t, _ = proj.shape
    nkb, tk = vt.shape[2], vt.shape[4]
    gw = ATT_GROUPS * HEAD_DIM
    return pl.pallas_call(
        functools.partial(_attn_kernel, tq=tq, tk=tk, nkb=nkb),
        grid=(b, ATT_KV_HEADS, t // tq),
        in_specs=[pl.BlockSpec((None, tq, gw), lambda i, h, j: (i, j, h)),
                  pl.BlockSpec((tq, HEAD_DIM), lambda i, h, j: (j, 0)),
                  pl.BlockSpec((tq, HEAD_DIM), lambda i, h, j: (j, 0)),
                  pl.BlockSpec((1, HEAD_DIM), lambda i, h, j: (0, 0)),
                  pl.BlockSpec((None, None, t, HEAD_DIM), lambda i, h, j: (i, h, 0, 0)),
                  pl.BlockSpec((None, None, nkb, HEAD_DIM, tk), lambda i, h, j: (i, h, 0, 0, 0))],
        out_specs=pl.BlockSpec((None, tq, gw), lambda i, h, j: (i, j, h)),
        out_shape=jax.ShapeDtypeStruct((b, t, ATT_Q_W), BF16),
        scratch_shapes=[pltpu.VMEM((2, tk, ATT_GROUPS * tq), F32)],
        compiler_params=_params(("parallel", "parallel", "arbitrary")),
        name="attention",
    )(proj, cos, sin, gq, kr, vt)


def _conv_kernel(xq_ref, xk_ref, pq_ref, pk_ref, nq_ref, nk_ref, w_ref, b_ref, q_ref, kt_ref, pad_ref, *, tc):
    r = pl.program_id(1)
    nr = pl.num_programs(1)
    halo = SUBLANES
    for c, (x_ref, prev_ref, next_ref) in enumerate(((xq_ref, pq_ref, nq_ref), (xk_ref, pk_ref, nk_ref))):
        cols = slice(c * M_QK_W, (c + 1) * M_QK_W)
        pad_ref[0:halo, cols] = jnp.where(r > 0, prev_ref[...], 0.0)
        pad_ref[halo:halo + tc, cols] = x_ref[...]
        pad_ref[halo + tc:2 * halo + tc, cols] = jnp.where(r < nr - 1, next_ref[...], 0.0)
    acc = jnp.broadcast_to(b_ref[...], (tc, 2 * M_QK_W))
    for k in range(M_CONV):
        lo = halo - M_CONV // 2 + k
        acc = acc + w_ref[k:k + 1, :] * pad_ref[lo:lo + tc, :]
    y = acc * jax.nn.sigmoid(acc)
    q_ref[...] = y[:, :M_QK_W].astype(BF16)
    for hd in range(M_HEADS):
        kh = y[:, M_QK_W + hd * M_QK_DIM:M_QK_W + (hd + 1) * M_QK_DIM] * (M_QK_DIM ** -0.5)
        kt_ref[hd] = kh.T.astype(BF16)


def _conv(proj, w, bias):
    b, t, _ = proj.shape
    cw = M_QK_W
    tc = _tile(t, TILE_CONV)
    hb = tc // SUBLANES
    c0 = COL_MQK // cw
    assert c0 * cw == COL_MQK
    last = t // SUBLANES - 1

    def prev_map(c):
        return lambda i, r: (i, jnp.maximum(r * hb - 1, 0), c)

    def next_map(c):
        return lambda i, r: (i, jnp.minimum((r + 1) * hb, last), c)

    return pl.pallas_call(
        functools.partial(_conv_kernel, tc=tc),
        grid=(b, t // tc),
        in_specs=[pl.BlockSpec((None, tc, cw), lambda i, r: (i, r, c0)),
                  pl.BlockSpec((None, tc, cw), lambda i, r: (i, r, c0 + 1)),
                  pl.BlockSpec((None, SUBLANES, cw), prev_map(c0)),
                  pl.BlockSpec((None, SUBLANES, cw), prev_map(c0 + 1)),
                  pl.BlockSpec((None, SUBLANES, cw), next_map(c0)),
                  pl.BlockSpec((None, SUBLANES, cw), next_map(c0 + 1)),
                  pl.BlockSpec((M_CONV, 2 * cw), lambda i, r: (0, 0)),
                  pl.BlockSpec((1, 2 * cw), lambda i, r: (0, 0))],
        out_specs=[pl.BlockSpec((None, tc, M_QK_W), lambda i, r: (i, r, 0)),
                   pl.BlockSpec((None, M_HEADS, M_QK_DIM, tc), lambda i, r: (i, 0, 0, r))],
        out_shape=[jax.ShapeDtypeStruct((b, t, M_QK_W), BF16),
                   jax.ShapeDtypeStruct((b, M_HEADS, M_QK_DIM, t), BF16)],
        scratch_shapes=[pltpu.VMEM((tc + 2 * SUBLANES, 2 * cw), F32)],
        compiler_params=_params(("parallel", "parallel")),
        name="conv",
    )(proj, proj, proj, proj, proj, proj, w, bias)


def _mlstm_kernel(*refs, reverse, bb):
    if reverse:
        (q_all, kt_all, vlo_ref, vhi_ref, gcol_ref, grow_ref, bi_ref, bf_ref, bcol_ref,
         hfw_all, olo_ref, ohi_ref, gn_ref, out_all, c_ref, m_ref) = refs
    else:
        (q_all, kt_all, vlo_ref, vhi_ref, gcol_ref, grow_ref, bi_ref, bf_ref, bcol_ref,
         out_all, c_ref, m_ref) = refs
    L = M_CHUNK

    @pl.when(pl.program_id(1) == 0)
    def _():
        c_ref[...] = jnp.zeros_like(c_ref)
        m_ref[...] = jnp.zeros_like(m_ref)

    ii = lax.broadcasted_iota(jnp.int32, (L, L), 0)
    jj = lax.broadcasted_iota(jnp.int32, (L, L), 1)
    mask = (jj >= ii) if reverse else (jj <= ii)
    tri = mask.astype(F32)
    tri_t = ((ii >= jj) if reverse else (ii <= jj)).astype(F32)
    kind = 2 * M_HEADS if reverse else 0
    lane = lax.broadcasted_iota(jnp.int32, (L, LANES), 1)
    row = lax.broadcasted_iota(jnp.int32, (L, LANES), 0)
    last_row = 0 if reverse else L - 1

    def running_max(a):
        s = 1
        while s < L:
            if reverse:
                shifted = jnp.where(row < L - s, pltpu.roll(a, L - s, axis=0), -jnp.inf)
            else:
                shifted = jnp.where(row >= s, pltpu.roll(a, s, axis=0), -jnp.inf)
            a = jnp.maximum(a, shifted)
            s *= 2
        return a

    m_all = m_ref[...]
    m_out = []
    for bi in range(bb):
        g_in = gcol_ref[bi, :, 0:LANES] + bi_ref[...]
        logf = _log_sigmoid(gcol_ref[bi, :, LANES:2 * LANES] + bf_ref[...])
        cum = jnp.dot(tri, logf, preferred_element_type=F32, precision=lax.Precision.HIGHEST)
        tot = jnp.sum(logf, axis=0, keepdims=True)
        m_prev = m_all[bi:bi + 1, :]
        u = jnp.maximum(running_max(g_in - cum), m_prev)
        u_last = u[last_row:last_row + 1, :]
        e_neg_m = jnp.exp(-(cum + u))
        s_prev_row = jnp.exp(m_prev - u_last)
        m_out.append(tot + u_last)
        gr = grow_ref[bi] + bcol_ref[...]
        cum_r = jnp.dot(_log_sigmoid(gr), tri_t, preferred_element_type=F32, precision=lax.Precision.HIGHEST)

        q_ref, out_ref = q_all.at[bi], out_all.at[bi]
        for hd in range(M_HEADS):
            st = bi * M_HEADS + hd
            li = kind + hd
            a_row = gr[li:li + 1, :] - cum_r[li + M_HEADS:li + M_HEADS + 1, :]
            u_b = jnp.broadcast_to(u[:, li:li + 1], (L, LANES))
            w_intra = jnp.where(mask, jnp.exp(a_row - u_b), 0.0)
            w_inter = jnp.exp(m_prev[:, li:li + 1] - u_b)
            q = q_ref[:, hd * M_QK_DIM:(hd + 1) * M_QK_DIM]
            kt = kt_all[bi, hd]
            v_ref = (vlo_ref if hd < M_HEADS // 2 else vhi_ref).at[bi]
            vo = (hd % (M_HEADS // 2)) * M_V_DIM
            ones_col = jnp.where(lane == li, 1.0, 0.0).astype(BF16)
            v_aug = jnp.concatenate([v_ref[:, vo:vo + M_V_DIM].astype(BF16), ones_col], axis=1)

            s = jnp.dot(q, kt, preferred_element_type=F32) * w_intra
            c_aug = c_ref[st]
            lhs = jnp.concatenate([s.astype(BF16), (q.astype(F32) * w_inter).astype(BF16)], axis=1)
            rhs = jnp.concatenate([v_aug, c_aug.astype(BF16)], axis=0)
            num = jnp.dot(lhs, rhs, preferred_element_type=F32)
            inv = 1.0 / jnp.maximum(jnp.abs(num[:, M_V_DIM:]), e_neg_m)
            h = num[:, :M_V_DIM] * inv[:, li:li + 1]

            w_g = jnp.exp(a_row - u_last[:, li:li + 1])
            kw = (kt.astype(F32) * w_g).astype(BF16)
            c_ref[st] = s_prev_row[:, li:li + 1] * c_aug + jnp.dot(kw, v_aug, preferred_element_type=F32)

            cs = slice(hd * M_V_DIM, (hd + 1) * M_V_DIM)
            if reverse:
                o_ref = (olo_ref if hd < M_HEADS // 2 else ohi_ref).at[bi]
                hs = hfw_all[bi, :, cs] + h
                y = _rms(hs, gn_ref[:, cs]) * jax.nn.sigmoid(o_ref[:, vo:vo + M_V_DIM])
                out_ref[:, cs] = y.astype(BF16)
            else:
                out_ref[:, cs] = h
    pad_rows = m_ref.shape[0] - bb
    m_ref[...] = jnp.concatenate(m_out + ([jnp.zeros((pad_rows, LANES), F32)] if pad_rows else []), axis=0)


def _mlstm(proj, qm, kt, gates_t, b_in, b_f, b_col, reverse, hfw=None, gn=None):
    b, t, _ = proj.shape
    nc = t // M_CHUNK
    half = M_V_W // 2
    bb = _tile(b, MLSTM_SEQS_PER_STEP)
    n_state = bb * M_HEADS
    assert bb <= SUBLANES and COL_MG % (2 * LANES) == 0

    def cidx(c):
        return nc - 1 - c if reverse else c

    in_specs = [pl.BlockSpec((bb, M_CHUNK, M_QK_W), lambda i, c: (i, cidx(c), 0)),
                pl.BlockSpec((bb, M_HEADS, M_QK_DIM, M_CHUNK), lambda i, c: (i, 0, 0, cidx(c))),
                pl.BlockSpec((bb, M_CHUNK, half), lambda i, c: (i, cidx(c), COL_MV // half)),
                pl.BlockSpec((bb, M_CHUNK, half), lambda i, c: (i, cidx(c), COL_MV // half + 1)),
                pl.BlockSpec((bb, M_CHUNK, 2 * LANES), lambda i, c: (i, cidx(c), COL_MG // (2 * LANES))),
                pl.BlockSpec((bb, M_GATE_W, M_CHUNK), lambda i, c: (i, 0, cidx(c))),
                pl.BlockSpec((1, LANES), lambda i, c: (0, 0)),
                pl.BlockSpec((1, LANES), lambda i, c: (0, 0)),
                pl.BlockSpec((M_GATE_W, 1), lambda i, c: (0, 0))]
    args = [qm, kt, proj, proj, proj, gates_t, b_in, b_f, b_col]
    if reverse:
        in_specs += [pl.BlockSpec((bb, M_CHUNK, M_V_W), lambda i, c: (i, cidx(c), 0)),
                     pl.BlockSpec((bb, M_CHUNK, half), lambda i, c: (i, cidx(c), COL_MO // half)),
                     pl.BlockSpec((bb, M_CHUNK, half), lambda i, c: (i, cidx(c), COL_MO // half + 1)),
                     pl.BlockSpec((1, M_V_W), lambda i, c: (0, 0))]
        args += [hfw, proj, proj, gn]
    return pl.pallas_call(
        functools.partial(_mlstm_kernel, reverse=reverse, bb=bb),
        grid=(b // bb, nc),
        in_specs=in_specs,
        out_specs=pl.BlockSpec((bb, M_CHUNK, M_V_W), lambda i, c: (i, cidx(c), 0)),
        out_shape=jax.ShapeDtypeStruct((b, t, M_V_W), BF16 if reverse else F32),
        scratch_shapes=[pltpu.VMEM((n_state, M_QK_DIM, M_AUG), F32),
                        pltpu.VMEM((SUBLANES, LANES), F32)],
        compiler_params=_params(("parallel", "arbitrary")),
        name="mlstm_bw" if reverse else "mlstm_fw",
    )(*args)


def _outproj_kernel(x_ref, mem_ref, att_ref, w_ref, g_ref, wr_ref, x1_ref, h_ref, aff_ref):
    half = mem_ref.shape[1]
    y = (jnp.dot(mem_ref[...], w_ref[0:half, :], preferred_element_type=F32)
         + jnp.dot(att_ref[...], w_ref[half:2 * half, :], preferred_element_type=F32))
    x1 = x_ref[...] + y
    x1_ref[...] = x1
    h = _rms(x1, g_ref[...])
    h_ref[...] = h
    h_hi = h.astype(BF16)
    h_lo = (h - h_hi.astype(F32)).astype(BF16)
    w = wr_ref[...]
    w_hi = w.astype(BF16)
    w_lo = (w - w_hi.astype(F32)).astype(BF16)
    nt_dims = (((1,), (1,)), ((), ()))
    logits = (lax.dot_general(w_hi, h_hi, nt_dims, preferred_element_type=F32)
              + lax.dot_general(w_lo, h_hi, nt_dims, preferred_element_type=F32)
              + lax.dot_general(w_hi, h_lo, nt_dims, preferred_element_type=F32))
    e = jnp.exp(logits - jnp.max(logits, axis=0, keepdims=True))
    aff_ref[...] = e / jnp.sum(e, axis=0, keepdims=True)


def _outproj(x, mem, att, w, layer, g, wr_t):
    n, d = x.shape
    half = mem.shape[1]
    tm = _tile(n, TILE_OUTPROJ)
    return pl.pallas_call(
        _outproj_kernel,
        grid=(n // tm,),
        in_specs=[pl.BlockSpec((tm, d), lambda i: (i, 0)),
                  pl.BlockSpec((tm, half), lambda i: (i, 0)),
                  pl.BlockSpec((tm, half), lambda i: (i, 0)),
                  pl.BlockSpec((None, 2 * half, d), lambda i: (layer, 0, 0), pipeline_mode=pl.Buffered(1)),
                  pl.BlockSpec((1, d), lambda i: (0, 0)),
                  pl.BlockSpec((N_EXPERTS, d), lambda i: (0, 0))],
        out_specs=[pl.BlockSpec((tm, d), lambda i: (i, 0)),
                   pl.BlockSpec((tm, d), lambda i: (i, 0)),
                   pl.BlockSpec((N_EXPERTS, tm), lambda i: (0, i))],
        out_shape=[jax.ShapeDtypeStruct((n, d), F32),
                   jax.ShapeDtypeStruct((n, d), F32),
                   jax.ShapeDtypeStruct((N_EXPERTS, n), F32)],
        compiler_params=_params(("parallel",)),
        name="outproj_router",
    )(x, mem, att, w, g, wr_t)


def _row_copy(src_hbm, src_row, dst, dst_row, sem):
    return pltpu.make_async_copy(src_hbm.at[pl.ds(src_row, 1), :], dst.at[pl.ds(dst_row, 1), :], sem)


def _ffn_kernel(idx_ref, idx_next_ref, h_hbm, gate_ref, wg_ref, wu_ref, wd_ref, o_ref, xf_ref, xb_ref, sem,
                *, tm):
    nt, nf = pl.num_programs(1), pl.num_programs(2)
    f = pl.program_id(2)
    tile = pl.program_id(0) * nt + pl.program_id(1)
    slot = tile % 2
    rows_per_step = tm // nf

    def tile_wait(s):
        pltpu.make_async_copy(h_hbm.at[pl.ds(0, tm), :], xf_ref.at[s], sem.at[s]).wait()

    @pl.when(f == 0)
    def _():
        @pl.when(tile == 0)
        def _():
            def issue(r, carry):
                _row_copy(h_hbm, idx_ref[0, 0, r], xf_ref.at[0], r, sem.at[0]).start()
                return carry
            lax.fori_loop(0, tm, issue, 0)
        tile_wait(slot)
        xb_ref[...] = xf_ref[slot].astype(BF16)
        o_ref[...] = jnp.zeros_like(o_ref)

    for r in range(rows_per_step):
        row = f * rows_per_step + r
        _row_copy(h_hbm, idx_next_ref[0, 0, row], xf_ref.at[1 - slot], row, sem.at[1 - slot]).start()

    x = xb_ref[...]
    a = jnp.dot(x, wg_ref[...], preferred_element_type=F32)
    u = jnp.dot(x, wu_ref[...], preferred_element_type=F32)
    hid = (a * jax.nn.sigmoid(a) * u).astype(BF16)
    o_ref[...] += jnp.dot(hid, wd_ref[...], preferred_element_type=F32) * gate_ref[...]

    @pl.when((tile == pl.num_programs(0) * nt - 1) & (f == nf - 1))
    def _():
        tile_wait(1 - slot)


def _ffn(h, idx, gate, wg, wu, wd, layer):
    n, d = h.shape
    e, cap = idx.shape
    ff = wg.shape[3]
    tm = _tile(cap, TILE_FFN_ROWS)
    tf = _tile(ff, TILE_FFN_HIDDEN)
    nt = cap // tm
    assert tm % (ff // tf) == 0
    idx3 = idx.reshape(e * nt, 1, tm)
    gate2 = gate.reshape(e * cap, 1)
    last = e * nt - 1
    return pl.pallas_call(
        functools.partial(_ffn_kernel, tm=tm),
        grid=(e, nt, ff // tf),
        in_specs=[pl.BlockSpec((1, 1, tm), lambda i, j, f: (i * nt + j, 0, 0), memory_space=pltpu.SMEM),
                  pl.BlockSpec((1, 1, tm), lambda i, j, f: (jnp.minimum(i * nt + j + 1, last), 0, 0),
                               memory_space=pltpu.SMEM),
                  pl.BlockSpec(memory_space=pl.ANY),
                  pl.BlockSpec((tm, 1), lambda i, j, f: (i * nt + j, 0)),
                  pl.BlockSpec((None, None, d, tf), lambda i, j, f: (layer, i, 0, f)),
                  pl.BlockSpec((None, None, d, tf), lambda i, j, f: (layer, i, 0, f)),
                  pl.BlockSpec((None, None, tf, d), lambda i, j, f: (layer, i, f, 0))],
        out_specs=pl.BlockSpec((tm, d), lambda i, j, f: (i * nt + j, 0)),
        out_shape=jax.ShapeDtypeStruct((e * cap, d), F32),
        scratch_shapes=[pltpu.VMEM((2, tm, d), F32), pltpu.VMEM((tm, d), BF16), pltpu.SemaphoreType.DMA((2,))],
        compiler_params=_params(("arbitrary", "arbitrary", "arbitrary")),
        name="expert_ffn",
    )(idx3, idx3, h, gate2, wg, wu, wd)


WAIT_CHUNK = 256
COMBINE_COLS = 256
ISSUE_UNROLL = 8


def _combine_kernel(ent_ref, off_ref, jmax_ref, x_ref, ye_hbm, g_ref, o_ref, buf_ref, sem, *, tb, final_norm):
    t = pl.program_id(0)
    d = buf_ref.shape[2]
    slot = t % 2

    def fill(tile, s):
        for j in range(MAX_PLANES):
            @pl.when(j < jmax_ref[tile])
            def _():
                buf_ref[s, j * tb:(j + 1) * tb, :] = jnp.zeros((tb, d), F32)
        lo = off_ref[tile]
        n_ent = off_ref[tile + 1] - lo

        def issue(p):
            ent = ent_ref[p]
            _row_copy(ye_hbm, ent & 0xFFFF, buf_ref.at[s], ent >> 16, sem.at[s]).start()

        def issue_group(q, carry):
            for r in range(ISSUE_UNROLL):
                issue(lo + q * ISSUE_UNROLL + r)
            return carry
        groups = n_ent // ISSUE_UNROLL
        lax.fori_loop(0, groups, issue_group, 0)

        def issue_rest(p, carry):
            issue(p)
            return carry
        lax.fori_loop(lo + groups * ISSUE_UNROLL, lo + n_ent, issue_rest, 0)

    @pl.when(t == 0)
    def _():
        fill(0, 0)

    @pl.when(t + 1 < pl.num_programs(0))
    def _():
        fill(t + 1, 1 - slot)

    def wait_rows(k):
        pltpu.make_async_copy(ye_hbm.at[pl.ds(0, k), :], buf_ref.at[slot, pl.ds(0, k), :], sem.at[slot]).wait()

    count = off_ref[t + 1] - off_ref[t]

    def bulk(p, carry):
        wait_rows(WAIT_CHUNK)
        return carry
    lax.fori_loop(0, count // WAIT_CHUNK, bulk, 0)
    k = WAIT_CHUNK // 2
    while k >= 1:
        @pl.when((count & k) != 0)
        def _():
            wait_rows(k)
        k //= 2

    jmax = jmax_ref[t]
    for c0 in range(0, d, COMBINE_COLS):
        cols = slice(c0, min(c0 + COMBINE_COLS, d))

        def add(j, acc):
            return acc + buf_ref[slot, pl.ds(pl.multiple_of(j * tb, tb), tb), cols]
        o_ref[:, cols] = lax.fori_loop(0, jmax, add, x_ref[:, cols])

    if final_norm:
        o_ref[...] = _rms(o_ref[...], g_ref[...])


def _combine(x1, ye, entries, offsets, jmax, g, tb, final_norm):
    n, d = x1.shape
    assert ye.shape[0] >= WAIT_CHUNK and MAX_PLANES * tb >= WAIT_CHUNK
    grid_spec = pltpu.PrefetchScalarGridSpec(
        num_scalar_prefetch=3,
        grid=(n // tb,),
        in_specs=[pl.BlockSpec((tb, d), lambda i, *_: (i, 0)),
                  pl.BlockSpec(memory_space=pl.ANY),
                  pl.BlockSpec((1, d), lambda i, *_: (0, 0))],
        out_specs=pl.BlockSpec((tb, d), lambda i, *_: (i, 0)),
        scratch_shapes=[pltpu.VMEM((2, MAX_PLANES * tb, d), F32), pltpu.SemaphoreType.DMA((2,))],
    )
    return pl.pallas_call(
        functools.partial(_combine_kernel, tb=tb, final_norm=final_norm),
        grid_spec=grid_spec,
        out_shape=jax.ShapeDtypeStruct((n, d), F32),
        compiler_params=_params(("arbitrary",)),
        name="combine",
    )(entries, offsets, jmax, x1, ye, g)


def _route(aff_t, tb):
    e, n = aff_t.shape
    cap = CAPACITY_FACTOR * n // e
    gate, idx = lax.top_k(aff_t, cap)
    total = e * cap
    pos = jnp.arange(total, dtype=jnp.int32)
    tok, src = lax.sort_key_val(idx.reshape(-1), pos)
    new_tok = jnp.concatenate([jnp.ones((1,), bool), tok[1:] != tok[:-1]])
    plane = pos - lax.cummax(jnp.where(new_tok, pos, 0))
    entries = src | ((plane * tb + tok % tb) << 16)
    tiles = jnp.arange(n // tb + 1, dtype=jnp.int32)
    offsets = jnp.sum(tok[None, :] < (tiles * tb)[:, None], axis=1, dtype=jnp.int32)
    in_tile = (tok // tb)[None, :] == tiles[:-1, None]
    jmax = jnp.max(jnp.where(in_tile, plane[None, :] + 1, 0), axis=1)
    return gate, idx, entries, offsets, jmax


def _rope_tables(t):
    pos = jnp.arange(t)
    row = (pos // GRID_W).astype(F32)
    col = (pos % GRID_W).astype(F32)
    inv = ROPE_THETA ** (-jnp.arange(0, AXIS_DIM, 2, dtype=F32) / AXIS_DIM)
    ar, ac = row[:, None] * inv[None, :], col[:, None] * inv[None, :]
    cos = jnp.concatenate([jnp.cos(ar), jnp.cos(ar), jnp.cos(ac), jnp.cos(ac)], axis=-1)
    sin = jnp.concatenate([-jnp.sin(ar), jnp.sin(ar), -jnp.sin(ac), jnp.sin(ac)], axis=-1)
    return cos, sin


def _trunk(x, p):
    b, t, d = x.shape
    n = b * t
    assert N_EXPERTS * (CAPACITY_FACTOR * n // N_EXPERTS) <= 1 << 16
    cos, sin = _rope_tables(t)
    tk = _tile(t, min(TILE_ATT_KV, max(t // ATT_MIN_KV_BLOCKS, M_CHUNK)))
    tq = _tile(t, TILE_ATT_Q * TILE_ATT_KV // tk)
    tb = _tile(n, TILE_COMBINE)
    xf = x.reshape(n, d)
    depth = p["w_in"].shape[0]
    for l in range(depth):
        proj, gates_t = _inproj(xf, p["norm1_g"][l], p["w_in"], l)
        proj = proj.reshape(b, t, IN_WIDTH_PAD)
        gates_t = jnp.swapaxes(gates_t.reshape(M_GATE_W, b, t), 0, 1)
        kr, vt = _kprep(proj, cos, sin, p["k_norm_g"][l], tk)
        att = _attention(proj, kr, vt, cos, sin, p["q_norm_g"][l], tq)
        qm, kt = _conv(proj, p["conv_w"][l], p["conv_b"][l])
        gate_bias = (p["b_in"][l], p["b_f"][l], p["b_col"][l])
        hfw = _mlstm(proj, qm, kt, gates_t, *gate_bias, reverse=False)
        mem = _mlstm(proj, qm, kt, gates_t, *gate_bias, reverse=True, hfw=hfw, gn=p["mlstm_norm_g"][l])
        x1, h2, aff_t = _outproj(xf, mem.reshape(n, M_V_W), att.reshape(n, ATT_Q_W), p["w_out"], l,
                                 p["norm2_g"][l], p["w_router_t"][l])
        gate, idx, entries, offsets, jmax = _route(aff_t, tb)
        ye = _ffn(h2, idx, gate, p["w_gate"], p["w_up"], p["w_down"], l)
        xf = _combine(x1, ye, entries, offsets, jmax, p["final_norm_g"], tb,
                      final_norm=(l == depth - 1))
    return xf.reshape(b, t, d)


def _prepare(norm1_g, w_in, conv_w, conv_b, b_gates, q_norm_g, k_norm_g, mlstm_norm_g, w_out,
             norm2_g, w_router, w_gate, w_up, w_down, final_norm_g):
    depth, d, _ = w_in.shape

    def forget_block(gates):
        pieces = []
        for kind in range(0, N_GATE_KINDS, 2):
            f_lo = (kind + 1) * M_HEADS
            pieces += [gates[..., f_lo:f_lo + M_HEADS], jnp.zeros(gates.shape[:-1] + (M_HEADS,), gates.dtype)]
        pieces.append(jnp.zeros(gates.shape[:-1] + (LANES - M_GATE_W,), gates.dtype))
        return jnp.concatenate(pieces, axis=-1)

    w_in_b = w_in.astype(BF16)
    first_block_pad = jnp.zeros((depth, d, COL_MG + LANES - IN_WIDTH), BF16)
    assert COL_MG + 2 * LANES == IN_WIDTH_PAD
    return {
        "norm1_g": norm1_g.reshape(depth, 1, d),
        "w_in": jnp.concatenate([w_in_b, first_block_pad, forget_block(w_in_b[..., COL_MG:])], axis=-1),
        "conv_w": conv_w,
        "conv_b": conv_b.reshape(depth, 1, -1),
        "b_in": jnp.pad(b_gates, ((0, 0), (0, LANES - M_GATE_W))).reshape(depth, 1, LANES),
        "b_f": forget_block(b_gates).reshape(depth, 1, LANES),
        "b_col": b_gates.reshape(depth, M_GATE_W, 1),
        "q_norm_g": q_norm_g.reshape(depth, 1, HEAD_DIM),
        "k_norm_g": k_norm_g.reshape(depth, 1, HEAD_DIM),
        "mlstm_norm_g": mlstm_norm_g.reshape(depth, 1, M_V_W),
        "w_out": w_out.astype(BF16),
        "norm2_g": norm2_g.reshape(depth, 1, d),
        "w_router_t": jnp.swapaxes(w_router, 1, 2),
        "w_gate": w_gate.astype(BF16),
        "w_up": w_up.astype(BF16),
        "w_down": w_down.astype(BF16),
        "final_norm_g": final_norm_g.reshape(1, d),
    }


def kernel(x_prompt, x_sample, norm1_g, w_in, conv_w, conv_b, b_gates, q_norm_g, k_norm_g, mlstm_norm_g,
           w_out, norm2_g, w_router, w_gate, w_up, w_down, final_norm_g):
    p = _prepare(norm1_g, w_in, conv_w, conv_b, b_gates, q_norm_g, k_norm_g, mlstm_norm_g, w_out,
                 norm2_g, w_router, w_gate, w_up, w_down, final_norm_g)
    return _trunk(x_prompt, p), _trunk(x_sample, p)
```

```python
import functools
import math

import jax
import jax.numpy as jnp
from jax import lax
from jax.experimental import pallas as pl
from jax.experimental.pallas import tpu as pltpu

F32 = jnp.float32
BF16 = jnp.bfloat16
EPS = 1e-6
LOG2E = 1.4426950408889634

LANES = 128
SUBLANES = 8
VMEM_BYTES_V7X = 64 * 1024 * 1024
VMEM_LIMIT = VMEM_BYTES_V7X - 8 * 1024 * 1024

GRID_W = 64
ATT_KV_HEADS = 2
ATT_GROUPS = 4
HEAD_DIM = 128
AXIS_DIM = HEAD_DIM // 2
ROPE_THETA = 10000.0
M_HEADS = 4
M_QK_DIM = 128
M_V_DIM = 256
M_CHUNK = 128
M_CONV = 5
N_GATE_KINDS = 4
N_EXPERTS = 16
CAPACITY_FACTOR = 2
ATT_Q_W = ATT_KV_HEADS * ATT_GROUPS * HEAD_DIM
ATT_KV_W = ATT_KV_HEADS * HEAD_DIM
M_QK_W = M_HEADS * M_QK_DIM
M_V_W = M_HEADS * M_V_DIM
M_GATE_W = N_GATE_KINDS * M_HEADS
COL_Q = 0
COL_K = COL_Q + ATT_Q_W
COL_V = COL_K + ATT_KV_W
COL_MQK = COL_V + ATT_KV_W
COL_MV = COL_MQK + 2 * M_QK_W
COL_MO = COL_MV + M_V_W
COL_MG = COL_MO + M_V_W
IN_WIDTH = COL_MG + M_GATE_W
IN_WIDTH_PAD = 4864
M_AUG = M_V_DIM + LANES
MAX_PLANES = N_EXPERTS

TILE_INPROJ = 256
TILE_ATT_Q = 256
TILE_ATT_KV = 1024
ATT_MIN_KV_BLOCKS = 4
TILE_CONV = 256
TILE_OUTPROJ = 512
TILE_FFN_ROWS = 512
TILE_FFN_HIDDEN = 512
TILE_COMBINE = 128
MLSTM_SEQS_PER_STEP = 2


def _tile(dim, pref):
    t = min(dim, pref)
    assert dim % t == 0, (dim, pref)
    return t


def _params(sem, vmem=VMEM_LIMIT):
    return pltpu.CompilerParams(dimension_semantics=sem, vmem_limit_bytes=vmem)


def _rms(x, g):
    return x * lax.rsqrt(jnp.mean(x * x, axis=-1, keepdims=True) + EPS) * g


def _log_sigmoid(x):
    return jnp.minimum(x, 0.0) - jnp.log1p(jnp.exp(-jnp.abs(x)))


def _rope(x, cos, sin):
    lane = lax.broadcasted_iota(jnp.int32, x.shape, 1)
    first = (lane % AXIS_DIM) < (AXIS_DIM // 2)
    swapped = jnp.where(first, pltpu.roll(x, LANES - AXIS_DIM // 2, axis=1),
                        pltpu.roll(x, AXIS_DIM // 2, axis=1))
    return x * cos + swapped * sin


def _inproj_kernel(x_ref, g_ref, w_ref, o_ref, gt_ref, *, col_chunk):
    h = _rms(x_ref[...], g_ref[...]).astype(BF16)
    width = o_ref.shape[1]
    for c0 in range(0, width, col_chunk):
        c1 = min(c0 + col_chunk, width)
        o_ref[:, c0:c1] = jnp.dot(h, w_ref[:, c0:c1], preferred_element_type=F32)
    gt_ref[...] = o_ref[:, COL_MG:COL_MG + LANES].T[:M_GATE_W, :]


def _inproj(x, g, w, layer):
    n, d = x.shape
    p = w.shape[2]
    tm = _tile(n, TILE_INPROJ)
    return pl.pallas_call(
        functools.partial(_inproj_kernel, col_chunk=512),
        grid=(n // tm,),
        in_specs=[pl.BlockSpec((tm, d), lambda i: (i, 0)),
                  pl.BlockSpec((1, d), lambda i: (0, 0)),
                  pl.BlockSpec((None, d, p), lambda i: (layer, 0, 0), pipeline_mode=pl.Buffered(1))],
        out_specs=[pl.BlockSpec((tm, p), lambda i: (i, 0)),
                   pl.BlockSpec((M_GATE_W, tm), lambda i: (0, i))],
        out_shape=[jax.ShapeDtypeStruct((n, p), F32),
                   jax.ShapeDtypeStruct((M_GATE_W, n), F32)],
        compiler_params=_params(("parallel",)),
        name="inproj",
    )(x, g, w)


def _kprep_kernel(k_ref, v_ref, cos_ref, sin_ref, g_ref, ko_ref, vt_ref):
    cos, sin, g = cos_ref[...], sin_ref[...], g_ref[...]
    for h in range(ATT_KV_HEADS):
        k = _rms(k_ref[:, h * HEAD_DIM:(h + 1) * HEAD_DIM], g)
        ko_ref[h] = _rope(k, cos, sin).astype(BF16)
        vt_ref[h, 0] = v_ref[:, h * HEAD_DIM:(h + 1) * HEAD_DIM].T.astype(BF16)


def _kprep(proj, cos, sin, gk, tk):
    b, t, _ = proj.shape
    nkb = t // tk
    return pl.pallas_call(
        _kprep_kernel,
        grid=(b, nkb),
        in_specs=[pl.BlockSpec((None, tk, ATT_KV_W), lambda i, j: (i, j, COL_K // ATT_KV_W)),
                  pl.BlockSpec((None, tk, ATT_KV_W), lambda i, j: (i, j, COL_V // ATT_KV_W)),
                  pl.BlockSpec((tk, HEAD_DIM), lambda i, j: (j, 0)),
                  pl.BlockSpec((tk, HEAD_DIM), lambda i, j: (j, 0)),
                  pl.BlockSpec((1, HEAD_DIM), lambda i, j: (0, 0))],
        out_specs=[pl.BlockSpec((None, ATT_KV_HEADS, tk, HEAD_DIM), lambda i, j: (i, 0, j, 0)),
                   pl.BlockSpec((None, ATT_KV_HEADS, 1, HEAD_DIM, tk), lambda i, j: (i, 0, j, 0, 0))],
        out_shape=[jax.ShapeDtypeStruct((b, ATT_KV_HEADS, t, HEAD_DIM), BF16),
                   jax.ShapeDtypeStruct((b, ATT_KV_HEADS, nkb, HEAD_DIM, tk), BF16)],
        compiler_params=_params(("parallel", "parallel")),
        name="kprep",
    )(proj, proj, cos, sin, gk)


def _attn_kernel(q_ref, cos_ref, sin_ref, g_ref, k_ref, vt_ref, o_ref, s_ref, *, tq, tk, nkb):
    cos, sin, g = cos_ref[...], sin_ref[...], g_ref[...]
    qscale = HEAD_DIM ** -0.5 * LOG2E
    qs = []
    for i in range(ATT_GROUPS):
        q = _rms(q_ref[:, i * HEAD_DIM:(i + 1) * HEAD_DIM], g)
        qs.append((_rope(q, cos, sin) * qscale).astype(BF16))
    q4 = jnp.concatenate(qs, axis=0)
    nq = ATT_GROUPS * tq

    def scores(j):
        kb = k_ref[pl.ds(pl.multiple_of(j * tk, tk), tk), :]
        return lax.dot_general(kb, q4, (((1,), (1,)), ((), ())), preferred_element_type=F32)

    def softmax_step(j, slot, carry):
        m, l, acc = carry
        s = s_ref[slot]
        m_new = jnp.maximum(m, jnp.max(s, axis=0, keepdims=True))
        alpha = jnp.exp2(m - m_new)
        p = jnp.exp2(s - m_new)
        l = alpha * l + jnp.sum(p, axis=0, keepdims=True)
        acc = alpha * acc + jnp.dot(vt_ref[j], p.astype(BF16), preferred_element_type=F32)
        return m_new, l, acc

    def pair(jj, carry):
        j = 2 * jj
        s_ref[1] = scores(j + 1)
        carry = softmax_step(j, 0, carry)
        s_ref[0] = scores(j + 2)
        return softmax_step(j + 1, 1, carry)

    s_ref[0] = scores(0)
    carry = (jnp.full((1, nq), -jnp.inf, F32), jnp.zeros((1, nq), F32), jnp.zeros((HEAD_DIM, nq), F32))
    if nkb > 1:
        assert nkb % 2 == 0
        carry = lax.fori_loop(0, nkb // 2 - 1, pair, carry)
        s_ref[1] = scores(nkb - 1)
        carry = softmax_step(nkb - 2, 0, carry)
        carry = softmax_step(nkb - 1, 1, carry)
    else:
        carry = softmax_step(0, 0, carry)
    _, l, acc = carry
    o = acc / l
    for i in range(ATT_GROUPS):
        o_ref[:, i * HEAD_DIM:(i + 1) * HEAD_DIM] = o[:, i * tq:(i + 1) * tq].T.astype(BF16)


def _attention(proj, kr, vt, cos, sin, gq, tq):
    b, t, _ = proj.shape
    nkb, tk = vt.shape[2], vt.shape[4]
    gw = ATT_GROUPS * HEAD_DIM
    return pl.pallas_call(
        functools.partial(_attn_kernel, tq=tq, tk=tk, nkb=nkb),
        grid=(b, ATT_KV_HEADS, t // tq),
        in_specs=[pl.BlockSpec((None, tq, gw), lambda i, h, j: (i, j, h)),
                  pl.BlockSpec((tq, HEAD_DIM), lambda i, h, j: (j, 0)),
                  pl.BlockSpec((tq, HEAD_DIM), lambda i, h, j: (j, 0)),
                  pl.BlockSpec((1, HEAD_DIM), lambda i, h, j: (0, 0)),
                  pl.BlockSpec((None, None, t, HEAD_DIM), lambda i, h, j: (i, h, 0, 0)),
                  pl.BlockSpec((None, None, nkb, HEAD_DIM, tk), lambda i, h, j: (i, h, 0, 0, 0))],
        out_specs=pl.BlockSpec((None, tq, gw), lambda i, h, j: (i, j, h)),
        out_shape=jax.ShapeDtypeStruct((b, t, ATT_Q_W), BF16),
        scratch_shapes=[pltpu.VMEM((2, tk, ATT_GROUPS * tq), F32)],
        compiler_params=_params(("parallel", "parallel", "arbitrary")),
        name="attention",
    )(proj, cos, sin, gq, kr, vt)


def _conv_kernel(xq_ref, xk_ref, pq_ref, pk_ref, nq_ref, nk_ref, w_ref, b_ref, q_ref, kt_ref, pad_ref, *, tc):
    r = pl.program_id(1)
    nr = pl.num_programs(1)
    halo = SUBLANES
    for c, (x_ref, prev_ref, next_ref) in enumerate(((xq_ref, pq_ref, nq_ref), (xk_ref, pk_ref, nk_ref))):
        cols = slice(c * M_QK_W, (c + 1) * M_QK_W)
        pad_ref[0:halo, cols] = jnp.where(r > 0, prev_ref[...], 0.0)
        pad_ref[halo:halo + tc, cols] = x_ref[...]
        pad_ref[halo + tc:2 * halo + tc, cols] = jnp.where(r < nr - 1, next_ref[...], 0.0)
    acc = jnp.broadcast_to(b_ref[...], (tc, 2 * M_QK_W))
    for k in range(M_CONV):
        lo = halo - M_CONV // 2 + k
        acc = acc + w_ref[k:k + 1, :] * pad_ref[lo:lo + tc, :]
    y = acc * jax.nn.sigmoid(acc)
    q_ref[...] = y[:, :M_QK_W].astype(BF16)
    for hd in range(M_HEADS):
        kh = y[:, M_QK_W + hd * M_QK_DIM:M_QK_W + (hd + 1) * M_QK_DIM] * (M_QK_DIM ** -0.5)
        kt_ref[hd] = kh.T.astype(BF16)


def _conv(proj, w, bias):
    b, t, _ = proj.shape
    cw = M_QK_W
    tc = _tile(t, TILE_CONV)
    hb = tc // SUBLANES
    c0 = COL_MQK // cw
    assert c0 * cw == COL_MQK
    last = t // SUBLANES - 1

    def prev_map(c):
        return lambda i, r: (i, jnp.maximum(r * hb - 1, 0), c)

    def next_map(c):
        return lambda i, r: (i, jnp.minimum((r + 1) * hb, last), c)

    return pl.pallas_call(
        functools.partial(_conv_kernel, tc=tc),
        grid=(b, t // tc),
        in_specs=[pl.BlockSpec((None, tc, cw), lambda i, r: (i, r, c0)),
                  pl.BlockSpec((None, tc, cw), lambda i, r: (i, r, c0 + 1)),
                  pl.BlockSpec((None, SUBLANES, cw), prev_map(c0)),
                  pl.BlockSpec((None, SUBLANES, cw), prev_map(c0 + 1)),
                  pl.BlockSpec((None, SUBLANES, cw), next_map(c0)),
                  pl.BlockSpec((None, SUBLANES, cw), next_map(c0 + 1)),
                  pl.BlockSpec((M_CONV, 2 * cw), lambda i, r: (0, 0)),
                  pl.BlockSpec((1, 2 * cw), lambda i, r: (0, 0))],
        out_specs=[pl.BlockSpec((None, tc, M_QK_W), lambda i, r: (i, r, 0)),
                   pl.BlockSpec((None, M_HEADS, M_QK_DIM, tc), lambda i, r: (i, 0, 0, r))],
        out_shape=[jax.ShapeDtypeStruct((b, t, M_QK_W), BF16),
                   jax.ShapeDtypeStruct((b, M_HEADS, M_QK_DIM, t), BF16)],
        scratch_shapes=[pltpu.VMEM((tc + 2 * SUBLANES, 2 * cw), F32)],
        compiler_params=_params(("parallel", "parallel")),
        name="conv",
    )(proj, proj, proj, proj, proj, proj, w, bias)


def _mlstm_kernel(*refs, reverse, bb):
    if reverse:
        (q_all, kt_all, vlo_ref, vhi_ref, gcol_ref, grow_ref, bi_ref, bf_ref, bcol_ref,
         hfw_all, olo_ref, ohi_ref, gn_ref, out_all, c_ref, m_ref) = refs
    else:
        (q_all, kt_all, vlo_ref, vhi_ref, gcol_ref, grow_ref, bi_ref, bf_ref, bcol_ref,
         out_all, c_ref, m_ref) = refs
    L = M_CHUNK

    @pl.when(pl.program_id(1) == 0)
    def _():
        c_ref[...] = jnp.zeros_like(c_ref)
        m_ref[...] = jnp.zeros_like(m_ref)

    ii = lax.broadcasted_iota(jnp.int32, (L, L), 0)
    jj = lax.broadcasted_iota(jnp.int32, (L, L), 1)
    mask = (jj >= ii) if reverse else (jj <= ii)
    tri = mask.astype(F32)
    tri_t = ((ii >= jj) if reverse else (ii <= jj)).astype(F32)
    kind = 2 * M_HEADS if reverse else 0
    lane = lax.broadcasted_iota(jnp.int32, (L, LANES), 1)
    row = lax.broadcasted_iota(jnp.int32, (L, LANES), 0)
    last_row = 0 if reverse else L - 1

    def running_max(a):
        s = 1
        while s < L:
            if reverse:
                shifted = jnp.where(row < L - s, pltpu.roll(a, L - s, axis=0), -jnp.inf)
            else:
                shifted = jnp.where(row >= s, pltpu.roll(a, s, axis=0), -jnp.inf)
            a = jnp.maximum(a, shifted)
            s *= 2
        return a

    m_all = m_ref[...]
    m_out = []
    for bi in range(bb):
        g_in = gcol_ref[bi, :, 0:LANES] + bi_ref[...]
        logf = _log_sigmoid(gcol_ref[bi, :, LANES:2 * LANES] + bf_ref[...])
        cum = jnp.dot(tri, logf, preferred_element_type=F32, precision=lax.Precision.HIGHEST)
        tot = jnp.sum(logf, axis=0, keepdims=True)
        m_prev = m_all[bi:bi + 1, :]
        u = jnp.maximum(running_max(g_in - cum), m_prev)
        u_last = u[last_row:last_row + 1, :]
        e_neg_m = jnp.exp(-(cum + u))
        s_prev_row = jnp.exp(m_prev - u_last)
        m_out.append(tot + u_last)
        gr = grow_ref[bi] + bcol_ref[...]
        cum_r = jnp.dot(_log_sigmoid(gr), tri_t, preferred_element_type=F32, precision=lax.Precision.HIGHEST)

        q_ref, out_ref = q_all.at[bi], out_all.at[bi]
        for hd in range(M_HEADS):
            st = bi * M_HEADS + hd
            li = kind + hd
            a_row = gr[li:li + 1, :] - cum_r[li + M_HEADS:li + M_HEADS + 1, :]
            u_b = jnp.broadcast_to(u[:, li:li + 1], (L, LANES))
            w_intra = jnp.where(mask, jnp.exp(a_row - u_b), 0.0)
            w_inter = jnp.exp(m_prev[:, li:li + 1] - u_b)
            q = q_ref[:, hd * M_QK_DIM:(hd + 1) * M_QK_DIM]
            kt = kt_all[bi, hd]
            v_ref = (vlo_ref if hd < M_HEADS // 2 else vhi_ref).at[bi]
            vo = (hd % (M_HEADS // 2)) * M_V_DIM
            ones_col = jnp.where(lane == li, 1.0, 0.0).astype(BF16)
            v_aug = jnp.concatenate([v_ref[:, vo:vo + M_V_DIM].astype(BF16), ones_col], axis=1)

            s = jnp.dot(q, kt, preferred_element_type=F32) * w_intra
            c_aug = c_ref[st]
            lhs = jnp.concatenate([s.astype(BF16), (q.astype(F32) * w_inter).astype(BF16)], axis=1)
            rhs = jnp.concatenate([v_aug, c_aug.astype(BF16)], axis=0)
            num = jnp.dot(lhs, rhs, preferred_element_type=F32)
            inv = 1.0 / jnp.maximum(jnp.abs(num[:, M_V_DIM:]), e_neg_m)
            h = num[:, :M_V_DIM] * inv[:, li:li + 1]

            w_g = jnp.exp(a_row - u_last[:, li:li + 1])
            kw = (kt.astype(F32) * w_g).astype(BF16)
            c_ref[st] = s_prev_row[:, li:li + 1] * c_aug + jnp.dot(kw, v_aug, preferred_element_type=F32)

            cs = slice(hd * M_V_DIM, (hd + 1) * M_V_DIM)
            if reverse:
                o_ref = (olo_ref if hd < M_HEADS // 2 else ohi_ref).at[bi]
                hs = hfw_all[bi, :, cs] + h
                y = _rms(hs, gn_ref[:, cs]) * jax.nn.sigmoid(o_ref[:, vo:vo + M_V_DIM])
                out_ref[:, cs] = y.astype(BF16)
            else:
                out_ref[:, cs] = h
    pad_rows = m_ref.shape[0] - bb
    m_ref[...] = jnp.concatenate(m_out + ([jnp.zeros((pad_rows, LANES), F32)] if pad_rows else []), axis=0)


def _mlstm(proj, qm, kt, gates_t, b_in, b_f, b_col, reverse, hfw=None, gn=None):
    b, t, _ = proj.shape
    nc = t // M_CHUNK
    half = M_V_W // 2
    bb = _tile(b, MLSTM_SEQS_PER_STEP)
    n_state = bb * M_HEADS
    assert bb <= SUBLANES and COL_MG % (2 * LANES) == 0

    def cidx(c):
        return nc - 1 - c if reverse else c

    in_specs = [pl.BlockSpec((bb, M_CHUNK, M_QK_W), lambda i, c: (i, cidx(c), 0)),
                pl.BlockSpec((bb, M_HEADS, M_QK_DIM, M_CHUNK), lambda i, c: (i, 0, 0, cidx(c))),
                pl.BlockSpec((bb, M_CHUNK, half), lambda i, c: (i, cidx(c), COL_MV // half)),
                pl.BlockSpec((bb, M_CHUNK, half), lambda i, c: (i, cidx(c), COL_MV // half + 1)),
                pl.BlockSpec((bb, M_CHUNK, 2 * LANES), lambda i, c: (i, cidx(c), COL_MG // (2 * LANES))),
                pl.BlockSpec((bb, M_GATE_W, M_CHUNK), lambda i, c: (i, 0, cidx(c))),
                pl.BlockSpec((1, LANES), lambda i, c: (0, 0)),
                pl.BlockSpec((1, LANES), lambda i, c: (0, 0)),
                pl.BlockSpec((M_GATE_W, 1), lambda i, c: (0, 0))]
    args = [qm, kt, proj, proj, proj, gates_t, b_in, b_f, b_col]
    if reverse:
        in_specs += [pl.BlockSpec((bb, M_CHUNK, M_V_W), lambda i, c: (i, cidx(c), 0)),
                     pl.BlockSpec((bb, M_CHUNK, half), lambda i, c: (i, cidx(c), COL_MO // half)),
                     pl.BlockSpec((bb, M_CHUNK, half), lambda i, c: (i, cidx(c), COL_MO // half + 1)),
                     pl.BlockSpec((1, M_V_W), lambda i, c: (0, 0))]
        args += [hfw, proj, proj, gn]
    return pl.pallas_call(
        functools.partial(_mlstm_kernel, reverse=reverse, bb=bb),
        grid=(b // bb, nc),
        in_specs=in_specs,
        out_specs=pl.BlockSpec((bb, M_CHUNK, M_V_W), lambda i, c: (i, cidx(c), 0)),
        out_shape=jax.ShapeDtypeStruct((b, t, M_V_W), BF16 if reverse else F32),
        scratch_shapes=[pltpu.VMEM((n_state, M_QK_DIM, M_AUG), F32),
                        pltpu.VMEM((SUBLANES, LANES), F32)],
        compiler_params=_params(("parallel", "arbitrary")),
        name="mlstm_bw" if reverse else "mlstm_fw",
    )(*args)


def _outproj_kernel(x_ref, mem_ref, att_ref, w_ref, g_ref, wr_ref, x1_ref, h_ref, aff_ref):
    half = mem_ref.shape[1]
    y = (jnp.dot(mem_ref[...], w_ref[0:half, :], preferred_element_type=F32)
         + jnp.dot(att_ref[...], w_ref[half:2 * half, :], preferred_element_type=F32))
    x1 = x_ref[...] + y
    x1_ref[...] = x1
    h = _rms(x1, g_ref[...])
    h_ref[...] = h
    h_hi = h.astype(BF16)
    h_lo = (h - h_hi.astype(F32)).astype(BF16)
    w = wr_ref[...]
    w_hi = w.astype(BF16)
    w_lo = (w - w_hi.astype(F32)).astype(BF16)
    nt_dims = (((1,), (1,)), ((), ()))
    logits = (lax.dot_general(w_hi, h_hi, nt_dims, preferred_element_type=F32)
              + lax.dot_general(w_lo, h_hi, nt_dims, preferred_element_type=F32)
              + lax.dot_general(w_hi, h_lo, nt_dims, preferred_element_type=F32))
    e = jnp.exp(logits - jnp.max(logits, axis=0, keepdims=True))
    aff_ref[...] = e / jnp.sum(e, axis=0, keepdims=True)


def _outproj(x, mem, att, w, layer, g, wr_t):
    n, d = x.shape
    half = mem.shape[1]
    tm = _tile(n, TILE_OUTPROJ)
    return pl.pallas_call(
        _outproj_kernel,
        grid=(n // tm,),
        in_specs=[pl.BlockSpec((tm, d), lambda i: (i, 0)),
                  pl.BlockSpec((tm, half), lambda i: (i, 0)),
                  pl.BlockSpec((tm, half), lambda i: (i, 0)),
                  pl.BlockSpec((None, 2 * half, d), lambda i: (layer, 0, 0), pipeline_mode=pl.Buffered(1)),
                  pl.BlockSpec((1, d), lambda i: (0, 0)),
                  pl.BlockSpec((N_EXPERTS, d), lambda i: (0, 0))],
        out_specs=[pl.BlockSpec((tm, d), lambda i: (i, 0)),
                   pl.BlockSpec((tm, d), lambda i: (i, 0)),
                   pl.BlockSpec((N_EXPERTS, tm), lambda i: (0, i))],
        out_shape=[jax.ShapeDtypeStruct((n, d), F32),
                   jax.ShapeDtypeStruct((n, d), F32),
                   jax.ShapeDtypeStruct((N_EXPERTS, n), F32)],
        compiler_params=_params(("parallel",)),
        name="outproj_router",
    )(x, mem, att, w, g, wr_t)


def _row_copy(src_hbm, src_row, dst, dst_row, sem):
    return pltpu.make_async_copy(src_hbm.at[pl.ds(src_row, 1), :], dst.at[pl.ds(dst_row, 1), :], sem)


def _ffn_kernel(idx_ref, idx_next_ref, h_hbm, gate_ref, wg_ref, wu_ref, wd_ref, o_ref, xf_ref, xb_ref, sem,
                *, tm):
    nt, nf = pl.num_programs(1), pl.num_programs(2)
    f = pl.program_id(2)
    tile = pl.program_id(0) * nt + pl.program_id(1)
    slot = tile % 2
    rows_per_step = tm // nf

    def tile_wait(s):
        pltpu.make_async_copy(h_hbm.at[pl.ds(0, tm), :], xf_ref.at[s], sem.at[s]).wait()

    @pl.when(f == 0)
    def _():
        @pl.when(tile == 0)
        def _():
            def issue(r, carry):
                _row_copy(h_hbm, idx_ref[0, 0, r], xf_ref.at[0], r, sem.at[0]).start()
                return carry
            lax.fori_loop(0, tm, issue, 0)
        tile_wait(slot)
        xb_ref[...] = xf_ref[slot].astype(BF16)
        o_ref[...] = jnp.zeros_like(o_ref)

    for r in range(rows_per_step):
        row = f * rows_per_step + r
        _row_copy(h_hbm, idx_next_ref[0, 0, row], xf_ref.at[1 - slot], row, sem.at[1 - slot]).start()

    x = xb_ref[...]
    a = jnp.dot(x, wg_ref[...], preferred_element_type=F32)
    u = jnp.dot(x, wu_ref[...], preferred_element_type=F32)
    hid = (a * jax.nn.sigmoid(a) * u).astype(BF16)
    o_ref[...] += jnp.dot(hid, wd_ref[...], preferred_element_type=F32) * gate_ref[...]

    @pl.when((tile == pl.num_programs(0) * nt - 1) & (f == nf - 1))
    def _():
        tile_wait(1 - slot)


def _ffn(h, idx, gate, wg, wu, wd, layer):
    n, d = h.shape
    e, cap = idx.shape
    ff = wg.shape[3]
    tm = _tile(cap, TILE_FFN_ROWS)
    tf = _tile(ff, TILE_FFN_HIDDEN)
    nt = cap // tm
    assert tm % (ff // tf) == 0
    idx3 = idx.reshape(e * nt, 1, tm)
    gate2 = gate.reshape(e * cap, 1)
    last = e * nt - 1
    return pl.pallas_call(
        functools.partial(_ffn_kernel, tm=tm),
        grid=(e, nt, ff // tf),
        in_specs=[pl.BlockSpec((1, 1, tm), lambda i, j, f: (i * nt + j, 0, 0), memory_space=pltpu.SMEM),
                  pl.BlockSpec((1, 1, tm), lambda i, j, f: (jnp.minimum(i * nt + j + 1, last), 0, 0),
                               memory_space=pltpu.SMEM),
                  pl.BlockSpec(memory_space=pl.ANY),
                  pl.BlockSpec((tm, 1), lambda i, j, f: (i * nt + j, 0)),
                  pl.BlockSpec((None, None, d, tf), lambda i, j, f: (layer, i, 0, f)),
                  pl.BlockSpec((None, None, d, tf), lambda i, j, f: (layer, i, 0, f)),
                  pl.BlockSpec((None, None, tf, d), lambda i, j, f: (layer, i, f, 0))],
        out_specs=pl.BlockSpec((tm, d), lambda i, j, f: (i * nt + j, 0)),
        out_shape=jax.ShapeDtypeStruct((e * cap, d), F32),
        scratch_shapes=[pltpu.VMEM((2, tm, d), F32), pltpu.VMEM((tm, d), BF16), pltpu.SemaphoreType.DMA((2,))],
        compiler_params=_params(("arbitrary", "arbitrary", "arbitrary")),
        name="expert_ffn",
    )(idx3, idx3, h, gate2, wg, wu, wd)


WAIT_CHUNK = 256
COMBINE_COLS = 256
ISSUE_UNROLL = 8
DMA_PRIORITY_THREADS = 2


def _combine_kernel(ent_ref, off_ref, jmax_ref, x_ref, ye_hbm, g_ref, o_ref, buf_ref, sem, *, tb, final_norm):
    t = pl.program_id(0)
    d = buf_ref.shape[2]
    slot = t % 2

    def fill(tile, s):
        for j in range(MAX_PLANES):
            @pl.when(j < jmax_ref[tile])
            def _():
                buf_ref[s, j * tb:(j + 1) * tb, :] = jnp.zeros((tb, d), F32)
        lo = off_ref[tile]
        n_ent = off_ref[tile + 1] - lo

        def issue(p, priority=0):
            ent = ent_ref[p]
            _row_copy(ye_hbm, ent & 0xFFFF, buf_ref.at[s], ent >> 16, sem.at[s]).start(priority=priority)

        def issue_group(q, carry):
            for r in range(ISSUE_UNROLL):
                issue(lo + q * ISSUE_UNROLL + r, priority=r % DMA_PRIORITY_THREADS)
            return carry
        groups = n_ent // ISSUE_UNROLL
        lax.fori_loop(0, groups, issue_group, 0)

        def issue_rest(p, carry):
            issue(p)
            return carry
        lax.fori_loop(lo + groups * ISSUE_UNROLL, lo + n_ent, issue_rest, 0)

    @pl.when(t == 0)
    def _():
        fill(0, 0)

    @pl.when(t + 1 < pl.num_programs(0))
    def _():
        fill(t + 1, 1 - slot)

    def wait_rows(k):
        pltpu.make_async_copy(ye_hbm.at[pl.ds(0, k), :], buf_ref.at[slot, pl.ds(0, k), :], sem.at[slot]).wait()

    count = off_ref[t + 1] - off_ref[t]

    def bulk(p, carry):
        wait_rows(WAIT_CHUNK)
        return carry
    lax.fori_loop(0, count // WAIT_CHUNK, bulk, 0)
    k = WAIT_CHUNK // 2
    while k >= 1:
        @pl.when((count & k) != 0)
        def _():
            wait_rows(k)
        k //= 2

    jmax = jmax_ref[t]
    for c0 in range(0, d, COMBINE_COLS):
        cols = slice(c0, min(c0 + COMBINE_COLS, d))

        def add(j, acc):
            return acc + buf_ref[slot, pl.ds(pl.multiple_of(j * tb, tb), tb), cols]
        o_ref[:, cols] = lax.fori_loop(0, jmax, add, x_ref[:, cols])

    if final_norm:
        o_ref[...] = _rms(o_ref[...], g_ref[...])


def _combine(x1, ye, entries, offsets, jmax, g, tb, final_norm):
    n, d = x1.shape
    assert ye.shape[0] >= WAIT_CHUNK and MAX_PLANES * tb >= WAIT_CHUNK
    grid_spec = pltpu.PrefetchScalarGridSpec(
        num_scalar_prefetch=3,
        grid=(n // tb,),
        in_specs=[pl.BlockSpec((tb, d), lambda i, *_: (i, 0)),
                  pl.BlockSpec(memory_space=pl.ANY),
                  pl.BlockSpec((1, d), lambda i, *_: (0, 0))],
        out_specs=pl.BlockSpec((tb, d), lambda i, *_: (i, 0)),
        scratch_shapes=[pltpu.VMEM((2, MAX_PLANES * tb, d), F32), pltpu.SemaphoreType.DMA((2,))],
    )
    return pl.pallas_call(
        functools.partial(_combine_kernel, tb=tb, final_norm=final_norm),
        grid_spec=grid_spec,
        out_shape=jax.ShapeDtypeStruct((n, d), F32),
        compiler_params=_params(("arbitrary",)),
        name="combine",
    )(entries, offsets, jmax, x1, ye, g)


def _route(aff_t, tb):
    e, n = aff_t.shape
    cap = CAPACITY_FACTOR * n // e
    gate, idx = lax.top_k(aff_t, cap)
    total = e * cap
    pos = jnp.arange(total, dtype=jnp.int32)
    tok, src = lax.sort_key_val(idx.reshape(-1), pos)
    new_tok = jnp.concatenate([jnp.ones((1,), bool), tok[1:] != tok[:-1]])
    plane = pos - lax.cummax(jnp.where(new_tok, pos, 0))
    entries = src | ((plane * tb + tok % tb) << 16)
    tiles = jnp.arange(n // tb + 1, dtype=jnp.int32)
    offsets = jnp.sum(tok[None, :] < (tiles * tb)[:, None], axis=1, dtype=jnp.int32)
    in_tile = (tok // tb)[None, :] == tiles[:-1, None]
    jmax = jnp.max(jnp.where(in_tile, plane[None, :] + 1, 0), axis=1)
    return gate, idx, entries, offsets, jmax


def _rope_tables(t):
    pos = jnp.arange(t)
    row = (pos // GRID_W).astype(F32)
    col = (pos % GRID_W).astype(F32)
    inv = ROPE_THETA ** (-jnp.arange(0, AXIS_DIM, 2, dtype=F32) / AXIS_DIM)
    ar, ac = row[:, None] * inv[None, :], col[:, None] * inv[None, :]
    cos = jnp.concatenate([jnp.cos(ar), jnp.cos(ar), jnp.cos(ac), jnp.cos(ac)], axis=-1)
    sin = jnp.concatenate([-jnp.sin(ar), jnp.sin(ar), -jnp.sin(ac), jnp.sin(ac)], axis=-1)
    return cos, sin


def _trunk(x, p):
    b, t, d = x.shape
    n = b * t
    assert N_EXPERTS * (CAPACITY_FACTOR * n // N_EXPERTS) <= 1 << 16
    cos, sin = _rope_tables(t)
    tk = _tile(t, min(TILE_ATT_KV, max(t // ATT_MIN_KV_BLOCKS, M_CHUNK)))
    tq = _tile(t, TILE_ATT_Q * TILE_ATT_KV // tk)
    tb = _tile(n, TILE_COMBINE)
    xf = x.reshape(n, d)
    depth = p["w_in"].shape[0]
    for l in range(depth):
        proj, gates_t = _inproj(xf, p["norm1_g"][l], p["w_in"], l)
        proj = proj.reshape(b, t, IN_WIDTH_PAD)
        gates_t = jnp.swapaxes(gates_t.reshape(M_GATE_W, b, t), 0, 1)
        kr, vt = _kprep(proj, cos, sin, p["k_norm_g"][l], tk)
        att = _attention(proj, kr, vt, cos, sin, p["q_norm_g"][l], tq)
        qm, kt = _conv(proj, p["conv_w"][l], p["conv_b"][l])
        gate_bias = (p["b_in"][l], p["b_f"][l], p["b_col"][l])
        hfw = _mlstm(proj, qm, kt, gates_t, *gate_bias, reverse=False)
        mem = _mlstm(proj, qm, kt, gates_t, *gate_bias, reverse=True, hfw=hfw, gn=p["mlstm_norm_g"][l])
        x1, h2, aff_t = _outproj(xf, mem.reshape(n, M_V_W), att.reshape(n, ATT_Q_W), p["w_out"], l,
                                 p["norm2_g"][l], p["w_router_t"][l])
        gate, idx, entries, offsets, jmax = _route(aff_t, tb)
        ye = _ffn(h2, idx, gate, p["w_gate"], p["w_up"], p["w_down"], l)
        xf = _combine(x1, ye, entries, offsets, jmax, p["final_norm_g"], tb,
                      final_norm=(l == depth - 1))
    return xf.reshape(b, t, d)


def _prepare(norm1_g, w_in, conv_w, conv_b, b_gates, q_norm_g, k_norm_g, mlstm_norm_g, w_out,
             norm2_g, w_router, w_gate, w_up, w_down, final_norm_g):
    depth, d, _ = w_in.shape

    def forget_block(gates):
        pieces = []
        for kind in range(0, N_GATE_KINDS, 2):
            f_lo = (kind + 1) * M_HEADS
            pieces += [gates[..., f_lo:f_lo + M_HEADS], jnp.zeros(gates.shape[:-1] + (M_HEADS,), gates.dtype)]
        pieces.append(jnp.zeros(gates.shape[:-1] + (LANES - M_GATE_W,), gates.dtype))
        return jnp.concatenate(pieces, axis=-1)

    w_in_b = w_in.astype(BF16)
    first_block_pad = jnp.zeros((depth, d, COL_MG + LANES - IN_WIDTH), BF16)
    assert COL_MG + 2 * LANES == IN_WIDTH_PAD
    return {
        "norm1_g": norm1_g.reshape(depth, 1, d),
        "w_in": jnp.concatenate([w_in_b, first_block_pad, forget_block(w_in_b[..., COL_MG:])], axis=-1),
        "conv_w": conv_w,
        "conv_b": conv_b.reshape(depth, 1, -1),
        "b_in": jnp.pad(b_gates, ((0, 0), (0, LANES - M_GATE_W))).reshape(depth, 1, LANES),
        "b_f": forget_block(b_gates).reshape(depth, 1, LANES),
        "b_col": b_gates.reshape(depth, M_GATE_W, 1),
        "q_norm_g": q_norm_g.reshape(depth, 1, HEAD_DIM),
        "k_norm_g": k_norm_g.reshape(depth, 1, HEAD_DIM),
        "mlstm_norm_g": mlstm_norm_g.reshape(depth, 1, M_V_W),
        "w_out": w_out.astype(BF16),
        "norm2_g": norm2_g.reshape(depth, 1, d),
        "w_router_t": jnp.swapaxes(w_router, 1, 2),
        "w_gate": w_gate.astype(BF16),
        "w_up": w_up.astype(BF16),
        "w_down": w_down.astype(BF16),
        "final_norm_g": final_norm_g.reshape(1, d),
    }


def kernel(x_prompt, x_sample, norm1_g, w_in, conv_w, conv_b, b_gates, q_norm_g, k_norm_g, mlstm_norm_g,
           w_out, norm2_g, w_router, w_gate, w_up, w_down, final_norm_g):
    p = _prepare(norm1_g, w_in, conv_w, conv_b, b_gates, q_norm_g, k_norm_g, mlstm_norm_g, w_out,
                 norm2_g, w_router, w_gate, w_up, w_down, final_norm_g)
    return _trunk(x_prompt, p), _trunk(x_sample, p)
```

```python
import functools
import math

import jax
import jax.numpy as jnp
from jax import lax
from jax.experimental import pallas as pl
from jax.experimental.pallas import tpu as pltpu

F32 = jnp.float32
BF16 = jnp.bfloat16
EPS = 1e-6
LOG2E = 1.4426950408889634

LANES = 128
SUBLANES = 8
VMEM_BYTES_V7X = 64 * 1024 * 1024
VMEM_LIMIT = VMEM_BYTES_V7X - 8 * 1024 * 1024

GRID_W = 64
ATT_KV_HEADS = 2
ATT_GROUPS = 4
HEAD_DIM = 128
AXIS_DIM = HEAD_DIM // 2
ROPE_THETA = 10000.0
M_HEADS = 4
M_QK_DIM = 128
M_V_DIM = 256
M_CHUNK = 128
M_CONV = 5
N_GATE_KINDS = 4
N_EXPERTS = 16
CAPACITY_FACTOR = 2
ATT_Q_W = ATT_KV_HEADS * ATT_GROUPS * HEAD_DIM
ATT_KV_W = ATT_KV_HEADS * HEAD_DIM
M_QK_W = M_HEADS * M_QK_DIM
M_V_W = M_HEADS * M_V_DIM
M_GATE_W = N_GATE_KINDS * M_HEADS
COL_Q = 0
COL_K = COL_Q + ATT_Q_W
COL_V = COL_K + ATT_KV_W
COL_MQK = COL_V + ATT_KV_W
COL_MV = COL_MQK + 2 * M_QK_W
COL_MO = COL_MV + M_V_W
COL_MG = COL_MO + M_V_W
IN_WIDTH = COL_MG + M_GATE_W
IN_WIDTH_PAD = 4864
M_AUG = M_V_DIM + LANES
MAX_PLANES = N_EXPERTS

TILE_INPROJ = 256
TILE_ATT_Q = 256
TILE_ATT_KV = 1024
ATT_MIN_KV_BLOCKS = 4
TILE_CONV = 512
TILE_OUTPROJ = 512
TILE_FFN_ROWS = 512
TILE_FFN_HIDDEN = 512
TILE_COMBINE = 128
MLSTM_SEQS_PER_STEP = 2


def _tile(dim, pref):
    t = min(dim, pref)
    assert dim % t == 0, (dim, pref)
    return t


def _params(sem, vmem=VMEM_LIMIT):
    return pltpu.CompilerParams(dimension_semantics=sem, vmem_limit_bytes=vmem)


def _rms(x, g):
    return x * lax.rsqrt(jnp.mean(x * x, axis=-1, keepdims=True) + EPS) * g


def _log_sigmoid(x):
    return jnp.minimum(x, 0.0) - jnp.log1p(jnp.exp(-jnp.abs(x)))


def _rope(x, cos, sin):
    lane = lax.broadcasted_iota(jnp.int32, x.shape, 1)
    first = (lane % AXIS_DIM) < (AXIS_DIM // 2)
    swapped = jnp.where(first, pltpu.roll(x, LANES - AXIS_DIM // 2, axis=1),
                        pltpu.roll(x, AXIS_DIM // 2, axis=1))
    return x * cos + swapped * sin


def _inproj_kernel(x_ref, g_ref, w_ref, o_ref, gt_ref, *, col_chunk):
    h = _rms(x_ref[...], g_ref[...]).astype(BF16)
    width = o_ref.shape[1]
    for c0 in range(0, width, col_chunk):
        c1 = min(c0 + col_chunk, width)
        o_ref[:, c0:c1] = jnp.dot(h, w_ref[:, c0:c1], preferred_element_type=F32)
    gt_ref[...] = o_ref[:, COL_MG:COL_MG + LANES].T[:M_GATE_W, :]


def _inproj(x, g, w, layer):
    n, d = x.shape
    p = w.shape[2]
    tm = _tile(n, TILE_INPROJ)
    return pl.pallas_call(
        functools.partial(_inproj_kernel, col_chunk=512),
        grid=(n // tm,),
        in_specs=[pl.BlockSpec((tm, d), lambda i: (i, 0)),
                  pl.BlockSpec((1, d), lambda i: (0, 0)),
                  pl.BlockSpec((None, d, p), lambda i: (layer, 0, 0), pipeline_mode=pl.Buffered(1))],
        out_specs=[pl.BlockSpec((tm, p), lambda i: (i, 0)),
                   pl.BlockSpec((M_GATE_W, tm), lambda i: (0, i))],
        out_shape=[jax.ShapeDtypeStruct((n, p), F32),
                   jax.ShapeDtypeStruct((M_GATE_W, n), F32)],
        compiler_params=_params(("parallel",)),
        name="inproj",
    )(x, g, w)


def _kprep_kernel(k_ref, v_ref, cos_ref, sin_ref, g_ref, ko_ref, vt_ref):
    cos, sin, g = cos_ref[...], sin_ref[...], g_ref[...]
    for h in range(ATT_KV_HEADS):
        k = _rms(k_ref[:, h * HEAD_DIM:(h + 1) * HEAD_DIM], g)
        ko_ref[h] = _rope(k, cos, sin).astype(BF16)
        vt_ref[h, 0] = v_ref[:, h * HEAD_DIM:(h + 1) * HEAD_DIM].T.astype(BF16)


def _kprep(proj, cos, sin, gk, tk):
    b, t, _ = proj.shape
    nkb = t // tk
    return pl.pallas_call(
        _kprep_kernel,
        grid=(b, nkb),
        in_specs=[pl.BlockSpec((None, tk, ATT_KV_W), lambda i, j: (i, j, COL_K // ATT_KV_W)),
                  pl.BlockSpec((None, tk, ATT_KV_W), lambda i, j: (i, j, COL_V // ATT_KV_W)),
                  pl.BlockSpec((tk, HEAD_DIM), lambda i, j: (j, 0)),
                  pl.BlockSpec((tk, HEAD_DIM), lambda i, j: (j, 0)),
                  pl.BlockSpec((1, HEAD_DIM), lambda i, j: (0, 0))],
        out_specs=[pl.BlockSpec((None, ATT_KV_HEADS, tk, HEAD_DIM), lambda i, j: (i, 0, j, 0)),
                   pl.BlockSpec((None, ATT_KV_HEADS, 1, HEAD_DIM, tk), lambda i, j: (i, 0, j, 0, 0))],
        out_shape=[jax.ShapeDtypeStruct((b, ATT_KV_HEADS, t, HEAD_DIM), BF16),
                   jax.ShapeDtypeStruct((b, ATT_KV_HEADS, nkb, HEAD_DIM, tk), BF16)],
        compiler_params=_params(("parallel", "parallel")),
        name="kprep",
    )(proj, proj, cos, sin, gk)


def _attn_kernel(q_ref, cos_ref, sin_ref, g_ref, k_ref, vt_ref, o_ref, s_ref, *, tq, tk, nkb):
    cos, sin, g = cos_ref[...], sin_ref[...], g_ref[...]
    qscale = HEAD_DIM ** -0.5 * LOG2E
    qs = []
    for i in range(ATT_GROUPS):
        q = _rms(q_ref[:, i * HEAD_DIM:(i + 1) * HEAD_DIM], g)
        qs.append((_rope(q, cos, sin) * qscale).astype(BF16))
    q4 = jnp.concatenate(qs, axis=0)
    nq = ATT_GROUPS * tq

    def scores(j):
        kb = k_ref[pl.ds(pl.multiple_of(j * tk, tk), tk), :]
        return lax.dot_general(kb, q4, (((1,), (1,)), ((), ())), preferred_element_type=F32)

    def softmax_step(j, slot, carry):
        m, l, acc = carry
        s = s_ref[slot]
        m_new = jnp.maximum(m, jnp.max(s, axis=0, keepdims=True))
        alpha = jnp.exp2(m - m_new)
        p = jnp.exp2(s - m_new)
        l = alpha * l + jnp.sum(p, axis=0, keepdims=True)
        acc = alpha * acc + jnp.dot(vt_ref[j], p.astype(BF16), preferred_element_type=F32)
        return m_new, l, acc

    def pair(jj, carry):
        j = 2 * jj
        s_ref[1] = scores(j + 1)
        carry = softmax_step(j, 0, carry)
        s_ref[0] = scores(j + 2)
        return softmax_step(j + 1, 1, carry)

    s_ref[0] = scores(0)
    carry = (jnp.full((1, nq), -jnp.inf, F32), jnp.zeros((1, nq), F32), jnp.zeros((HEAD_DIM, nq), F32))
    if nkb > 1:
        assert nkb % 2 == 0
        carry = lax.fori_loop(0, nkb // 2 - 1, pair, carry)
        s_ref[1] = scores(nkb - 1)
        carry = softmax_step(nkb - 2, 0, carry)
        carry = softmax_step(nkb - 1, 1, carry)
    else:
        carry = softmax_step(0, 0, carry)
    _, l, acc = carry
    o = acc / l
    for i in range(ATT_GROUPS):
        o_ref[:, i * HEAD_DIM:(i + 1) * HEAD_DIM] = o[:, i * tq:(i + 1) * tq].T.astype(BF16)


def _attention(proj, kr, vt, cos, sin, gq, tq):
    b, t, _ = proj.shape
    nkb, tk = vt.shape[2], vt.shape[4]
    gw = ATT_GROUPS * HEAD_DIM
    return pl.pallas_call(
        functools.partial(_attn_kernel, tq=tq, tk=tk, nkb=nkb),
        grid=(b, ATT_KV_HEADS, t // tq),
        in_specs=[pl.BlockSpec((None, tq, gw), lambda i, h, j: (i, j, h)),
                  pl.BlockSpec((tq, HEAD_DIM), lambda i, h, j: (j, 0)),
                  pl.BlockSpec((tq, HEAD_DIM), lambda i, h, j: (j, 0)),
                  pl.BlockSpec((1, HEAD_DIM), lambda i, h, j: (0, 0)),
                  pl.BlockSpec((None, None, t, HEAD_DIM), lambda i, h, j: (i, h, 0, 0)),
                  pl.BlockSpec((None, None, nkb, HEAD_DIM, tk), lambda i, h, j: (i, h, 0, 0, 0))],
        out_specs=pl.BlockSpec((None, tq, gw), lambda i, h, j: (i, j, h)),
        out_shape=jax.ShapeDtypeStruct((b, t, ATT_Q_W), BF16),
        scratch_shapes=[pltpu.VMEM((2, tk, ATT_GROUPS * tq), F32)],
        compiler_params=_params(("parallel", "parallel", "arbitrary")),
        name="attention",
    )(proj, cos, sin, gq, kr, vt)


def _conv_kernel(xq_ref, xk_ref, pq_ref, pk_ref, nq_ref, nk_ref, w_ref, b_ref, q_ref, kt_ref, pad_ref, *, tc):
    r = pl.program_id(1)
    nr = pl.num_programs(1)
    halo = SUBLANES
    for c, (x_ref, prev_ref, next_ref) in enumerate(((xq_ref, pq_ref, nq_ref), (xk_ref, pk_ref, nk_ref))):
        cols = slice(c * M_QK_W, (c + 1) * M_QK_W)
        pad_ref[0:halo, cols] = jnp.where(r > 0, prev_ref[...], 0.0)
        pad_ref[halo:halo + tc, cols] = x_ref[...]
        pad_ref[halo + tc:2 * halo + tc, cols] = jnp.where(r < nr - 1, next_ref[...], 0.0)
    acc = jnp.broadcast_to(b_ref[...], (tc, 2 * M_QK_W))
    for k in range(M_CONV):
        lo = halo - M_CONV // 2 + k
        acc = acc + w_ref[k:k + 1, :] * pad_ref[lo:lo + tc, :]
    y = acc * jax.nn.sigmoid(acc)
    q_ref[...] = y[:, :M_QK_W].astype(BF16)
    for hd in range(M_HEADS):
        kh = y[:, M_QK_W + hd * M_QK_DIM:M_QK_W + (hd + 1) * M_QK_DIM] * (M_QK_DIM ** -0.5)
        kt_ref[hd] = kh.T.astype(BF16)


def _conv(proj, w, bias):
    b, t, _ = proj.shape
    cw = M_QK_W
    tc = _tile(t, TILE_CONV)
    hb = tc // SUBLANES
    c0 = COL_MQK // cw
    assert c0 * cw == COL_MQK
    last = t // SUBLANES - 1

    def prev_map(c):
        return lambda i, r: (i, jnp.maximum(r * hb - 1, 0), c)

    def next_map(c):
        return lambda i, r: (i, jnp.minimum((r + 1) * hb, last), c)

    return pl.pallas_call(
        functools.partial(_conv_kernel, tc=tc),
        grid=(b, t // tc),
        in_specs=[pl.BlockSpec((None, tc, cw), lambda i, r: (i, r, c0)),
                  pl.BlockSpec((None, tc, cw), lambda i, r: (i, r, c0 + 1)),
                  pl.BlockSpec((None, SUBLANES, cw), prev_map(c0)),
                  pl.BlockSpec((None, SUBLANES, cw), prev_map(c0 + 1)),
                  pl.BlockSpec((None, SUBLANES, cw), next_map(c0)),
                  pl.BlockSpec((None, SUBLANES, cw), next_map(c0 + 1)),
                  pl.BlockSpec((M_CONV, 2 * cw), lambda i, r: (0, 0)),
                  pl.BlockSpec((1, 2 * cw), lambda i, r: (0, 0))],
        out_specs=[pl.BlockSpec((None, tc, M_QK_W), lambda i, r: (i, r, 0)),
                   pl.BlockSpec((None, M_HEADS, M_QK_DIM, tc), lambda i, r: (i, 0, 0, r))],
        out_shape=[jax.ShapeDtypeStruct((b, t, M_QK_W), BF16),
                   jax.ShapeDtypeStruct((b, M_HEADS, M_QK_DIM, t), BF16)],
        scratch_shapes=[pltpu.VMEM((tc + 2 * SUBLANES, 2 * cw), F32)],
        compiler_params=_params(("parallel", "parallel")),
        name="conv",
    )(proj, proj, proj, proj, proj, proj, w, bias)


def _mlstm_kernel(*refs, reverse, bb):
    if reverse:
        (q_all, kt_all, vlo_ref, vhi_ref, gcol_ref, grow_ref, bi_ref, bf_ref, bcol_ref,
         hfw_all, olo_ref, ohi_ref, gn_ref, out_all, c_ref, m_ref) = refs
    else:
        (q_all, kt_all, vlo_ref, vhi_ref, gcol_ref, grow_ref, bi_ref, bf_ref, bcol_ref,
         out_all, c_ref, m_ref) = refs
    L = M_CHUNK

    @pl.when(pl.program_id(1) == 0)
    def _():
        c_ref[...] = jnp.zeros_like(c_ref)
        m_ref[...] = jnp.zeros_like(m_ref)

    ii = lax.broadcasted_iota(jnp.int32, (L, L), 0)
    jj = lax.broadcasted_iota(jnp.int32, (L, L), 1)
    mask = (jj >= ii) if reverse else (jj <= ii)
    tri = mask.astype(F32)
    tri_t = ((ii >= jj) if reverse else (ii <= jj)).astype(F32)
    kind = 2 * M_HEADS if reverse else 0
    lane = lax.broadcasted_iota(jnp.int32, (L, LANES), 1)
    row = lax.broadcasted_iota(jnp.int32, (L, LANES), 0)
    last_row = 0 if reverse else L - 1

    def running_max(a):
        s = 1
        while s < L:
            if reverse:
                shifted = jnp.where(row < L - s, pltpu.roll(a, L - s, axis=0), -jnp.inf)
            else:
                shifted = jnp.where(row >= s, pltpu.roll(a, s, axis=0), -jnp.inf)
            a = jnp.maximum(a, shifted)
            s *= 2
        return a

    m_all = m_ref[...]
    m_out = []
    for bi in range(bb):
        g_in = gcol_ref[bi, :, 0:LANES] + bi_ref[...]
        logf = _log_sigmoid(gcol_ref[bi, :, LANES:2 * LANES] + bf_ref[...])
        cum = jnp.dot(tri, logf, preferred_element_type=F32, precision=lax.Precision.HIGHEST)
        tot = jnp.sum(logf, axis=0, keepdims=True)
        m_prev = m_all[bi:bi + 1, :]
        u = jnp.maximum(running_max(g_in - cum), m_prev)
        u_last = u[last_row:last_row + 1, :]
        e_neg_m = jnp.exp(-(cum + u))
        s_prev_row = jnp.exp(m_prev - u_last)
        m_out.append(tot + u_last)
        gr = grow_ref[bi] + bcol_ref[...]
        cum_r = jnp.dot(_log_sigmoid(gr), tri_t, preferred_element_type=F32, precision=lax.Precision.HIGHEST)

        q_ref, out_ref = q_all.at[bi], out_all.at[bi]
        for hd in range(M_HEADS):
            st = bi * M_HEADS + hd
            li = kind + hd
            a_row = gr[li:li + 1, :] - cum_r[li + M_HEADS:li + M_HEADS + 1, :]
            u_b = jnp.broadcast_to(u[:, li:li + 1], (L, LANES))
            w_intra = jnp.where(mask, jnp.exp(a_row - u_b), 0.0)
            w_inter = jnp.exp(m_prev[:, li:li + 1] - u_b)
            q = q_ref[:, hd * M_QK_DIM:(hd + 1) * M_QK_DIM]
            kt = kt_all[bi, hd]
            v_ref = (vlo_ref if hd < M_HEADS // 2 else vhi_ref).at[bi]
            vo = (hd % (M_HEADS // 2)) * M_V_DIM
            ones_col = jnp.where(lane == li, 1.0, 0.0).astype(BF16)
            v_aug = jnp.concatenate([v_ref[:, vo:vo + M_V_DIM].astype(BF16), ones_col], axis=1)

            s = jnp.dot(q, kt, preferred_element_type=F32) * w_intra
            c_aug = c_ref[st]
            lhs = jnp.concatenate([s.astype(BF16), (q.astype(F32) * w_inter).astype(BF16)], axis=1)
            rhs = jnp.concatenate([v_aug, c_aug.astype(BF16)], axis=0)
            num = jnp.dot(lhs, rhs, preferred_element_type=F32)
            inv = 1.0 / jnp.maximum(jnp.abs(num[:, M_V_DIM:]), e_neg_m)
            h = num[:, :M_V_DIM] * inv[:, li:li + 1]

            w_g = jnp.exp(a_row - u_last[:, li:li + 1])
            kw = (kt.astype(F32) * w_g).astype(BF16)
            c_ref[st] = s_prev_row[:, li:li + 1] * c_aug + jnp.dot(kw, v_aug, preferred_element_type=F32)

            cs = slice(hd * M_V_DIM, (hd + 1) * M_V_DIM)
            if reverse:
                o_ref = (olo_ref if hd < M_HEADS // 2 else ohi_ref).at[bi]
                hs = hfw_all[bi, :, cs] + h
                y = _rms(hs, gn_ref[:, cs]) * jax.nn.sigmoid(o_ref[:, vo:vo + M_V_DIM])
                out_ref[:, cs] = y.astype(BF16)
            else:
                out_ref[:, cs] = h
    pad_rows = m_ref.shape[0] - bb
    m_ref[...] = jnp.concatenate(m_out + ([jnp.zeros((pad_rows, LANES), F32)] if pad_rows else []), axis=0)


def _mlstm(proj, qm, kt, gates_t, b_in, b_f, b_col, reverse, hfw=None, gn=None):
    b, t, _ = proj.shape
    nc = t // M_CHUNK
    half = M_V_W // 2
    bb = _tile(b, MLSTM_SEQS_PER_STEP)
    n_state = bb * M_HEADS
    assert bb <= SUBLANES and COL_MG % (2 * LANES) == 0

    def cidx(c):
        return nc - 1 - c if reverse else c

    in_specs = [pl.BlockSpec((bb, M_CHUNK, M_QK_W), lambda i, c: (i, cidx(c), 0)),
                pl.BlockSpec((bb, M_HEADS, M_QK_DIM, M_CHUNK), lambda i, c: (i, 0, 0, cidx(c))),
                pl.BlockSpec((bb, M_CHUNK, half), lambda i, c: (i, cidx(c), COL_MV // half)),
                pl.BlockSpec((bb, M_CHUNK, half), lambda i, c: (i, cidx(c), COL_MV // half + 1)),
                pl.BlockSpec((bb, M_CHUNK, 2 * LANES), lambda i, c: (i, cidx(c), COL_MG // (2 * LANES))),
                pl.BlockSpec((bb, M_GATE_W, M_CHUNK), lambda i, c: (i, 0, cidx(c))),
                pl.BlockSpec((1, LANES), lambda i, c: (0, 0)),
                pl.BlockSpec((1, LANES), lambda i, c: (0, 0)),
                pl.BlockSpec((M_GATE_W, 1), lambda i, c: (0, 0))]
    args = [qm, kt, proj, proj, proj, gates_t, b_in, b_f, b_col]
    if reverse:
        in_specs += [pl.BlockSpec((bb, M_CHUNK, M_V_W), lambda i, c: (i, cidx(c), 0)),
                     pl.BlockSpec((bb, M_CHUNK, half), lambda i, c: (i, cidx(c), COL_MO // half)),
                     pl.BlockSpec((bb, M_CHUNK, half), lambda i, c: (i, cidx(c), COL_MO // half + 1)),
                     pl.BlockSpec((1, M_V_W), lambda i, c: (0, 0))]
        args += [hfw, proj, proj, gn]
    return pl.pallas_call(
        functools.partial(_mlstm_kernel, reverse=reverse, bb=bb),
        grid=(b // bb, nc),
        in_specs=in_specs,
        out_specs=pl.BlockSpec((bb, M_CHUNK, M_V_W), lambda i, c: (i, cidx(c), 0)),
        out_shape=jax.ShapeDtypeStruct((b, t, M_V_W), BF16 if reverse else F32),
        scratch_shapes=[pltpu.VMEM((n_state, M_QK_DIM, M_AUG), F32),
                        pltpu.VMEM((SUBLANES, LANES), F32)],
        compiler_params=_params(("parallel", "arbitrary")),
        name="mlstm_bw" if reverse else "mlstm_fw",
    )(*args)


def _outproj_kernel(x_ref, mem_ref, att_ref, w_ref, g_ref, wr_ref, x1_ref, h_ref, aff_ref):
    half = mem_ref.shape[1]
    y = (jnp.dot(mem_ref[...], w_ref[0:half, :], preferred_element_type=F32)
         + jnp.dot(att_ref[...], w_ref[half:2 * half, :], preferred_element_type=F32))
    x1 = x_ref[...] + y
    x1_ref[...] = x1
    h = _rms(x1, g_ref[...])
    h_ref[...] = h
    h_hi = h.astype(BF16)
    h_lo = (h - h_hi.astype(F32)).astype(BF16)
    w = wr_ref[...]
    w_hi = w.astype(BF16)
    w_lo = (w - w_hi.astype(F32)).astype(BF16)
    nt_dims = (((1,), (1,)), ((), ()))
    logits = (lax.dot_general(w_hi, h_hi, nt_dims, preferred_element_type=F32)
              + lax.dot_general(w_lo, h_hi, nt_dims, preferred_element_type=F32)
              + lax.dot_general(w_hi, h_lo, nt_dims, preferred_element_type=F32))
    e = jnp.exp(logits - jnp.max(logits, axis=0, keepdims=True))
    aff_ref[...] = e / jnp.sum(e, axis=0, keepdims=True)


def _outproj(x, mem, att, w, layer, g, wr_t):
    n, d = x.shape
    half = mem.shape[1]
    tm = _tile(n, TILE_OUTPROJ)
    return pl.pallas_call(
        _outproj_kernel,
        grid=(n // tm,),
        in_specs=[pl.BlockSpec((tm, d), lambda i: (i, 0)),
                  pl.BlockSpec((tm, half), lambda i: (i, 0)),
                  pl.BlockSpec((tm, half), lambda i: (i, 0)),
                  pl.BlockSpec((None, 2 * half, d), lambda i: (layer, 0, 0), pipeline_mode=pl.Buffered(1)),
                  pl.BlockSpec((1, d), lambda i: (0, 0)),
                  pl.BlockSpec((N_EXPERTS, d), lambda i: (0, 0))],
        out_specs=[pl.BlockSpec((tm, d), lambda i: (i, 0)),
                   pl.BlockSpec((tm, d), lambda i: (i, 0)),
                   pl.BlockSpec((N_EXPERTS, tm), lambda i: (0, i))],
        out_shape=[jax.ShapeDtypeStruct((n, d), F32),
                   jax.ShapeDtypeStruct((n, d), F32),
                   jax.ShapeDtypeStruct((N_EXPERTS, n), F32)],
        compiler_params=_params(("parallel",)),
        name="outproj_router",
    )(x, mem, att, w, g, wr_t)


def _row_copy(src_hbm, src_row, dst, dst_row, sem):
    return pltpu.make_async_copy(src_hbm.at[pl.ds(src_row, 1), :], dst.at[pl.ds(dst_row, 1), :], sem)


def _pack_bf16_pairs(x):
    w = x.shape[1] // 2
    lo = lax.bitcast_convert_type(x[:, :w].astype(BF16).astype(F32), jnp.uint32)
    hi = lax.bitcast_convert_type(x[:, w:].astype(BF16).astype(F32), jnp.uint32)
    return (lo >> 16) | (hi & jnp.uint32(0xFFFF0000))


def _unpack_bf16_pairs(p):
    return (lax.bitcast_convert_type(p << 16, F32),
            lax.bitcast_convert_type(p & jnp.uint32(0xFFFF0000), F32))


def _ffn_kernel(idx_ref, idx_next_ref, h_hbm, gate_ref, wg_ref, wu_ref, wd_ref, o_ref, xf_ref, xb_ref, acc_ref,
                sem, *, tm):
    nt, nf = pl.num_programs(1), pl.num_programs(2)
    f = pl.program_id(2)
    tile = pl.program_id(0) * nt + pl.program_id(1)
    slot = tile % 2
    rows_per_step = tm // nf

    def tile_wait(s):
        pltpu.make_async_copy(h_hbm.at[pl.ds(0, tm), :], xf_ref.at[s], sem.at[s]).wait()

    @pl.when(f == 0)
    def _():
        @pl.when(tile == 0)
        def _():
            def issue(r, carry):
                _row_copy(h_hbm, idx_ref[0, 0, r], xf_ref.at[0], r, sem.at[0]).start()
                return carry
            lax.fori_loop(0, tm, issue, 0)
        tile_wait(slot)
        xb_ref[...] = xf_ref[slot].astype(BF16)
        acc_ref[...] = jnp.zeros_like(acc_ref)

    for r in range(rows_per_step):
        row = f * rows_per_step + r
        _row_copy(h_hbm, idx_next_ref[0, 0, row], xf_ref.at[1 - slot], row, sem.at[1 - slot]).start()

    x = xb_ref[...]
    a = jnp.dot(x, wg_ref[...], preferred_element_type=F32)
    u = jnp.dot(x, wu_ref[...], preferred_element_type=F32)
    hid = (a * jax.nn.sigmoid(a) * u).astype(BF16)
    acc_ref[...] += jnp.dot(hid, wd_ref[...], preferred_element_type=F32) * gate_ref[...]

    @pl.when(f == nf - 1)
    def _():
        o_ref[...] = _pack_bf16_pairs(acc_ref[...])

    @pl.when((tile == pl.num_programs(0) * nt - 1) & (f == nf - 1))
    def _():
        tile_wait(1 - slot)


def _ffn(h, idx, gate, wg, wu, wd, layer):
    n, d = h.shape
    e, cap = idx.shape
    ff = wg.shape[3]
    tm = _tile(cap, TILE_FFN_ROWS)
    tf = _tile(ff, TILE_FFN_HIDDEN)
    nt = cap // tm
    assert tm % (ff // tf) == 0
    idx3 = idx.reshape(e * nt, 1, tm)
    gate2 = gate.reshape(e * cap, 1)
    last = e * nt - 1
    return pl.pallas_call(
        functools.partial(_ffn_kernel, tm=tm),
        grid=(e, nt, ff // tf),
        in_specs=[pl.BlockSpec((1, 1, tm), lambda i, j, f: (i * nt + j, 0, 0), memory_space=pltpu.SMEM),
                  pl.BlockSpec((1, 1, tm), lambda i, j, f: (jnp.minimum(i * nt + j + 1, last), 0, 0),
                               memory_space=pltpu.SMEM),
                  pl.BlockSpec(memory_space=pl.ANY),
                  pl.BlockSpec((tm, 1), lambda i, j, f: (i * nt + j, 0)),
                  pl.BlockSpec((None, None, d, tf), lambda i, j, f: (layer, i, 0, f)),
                  pl.BlockSpec((None, None, d, tf), lambda i, j, f: (layer, i, 0, f)),
                  pl.BlockSpec((None, None, tf, d), lambda i, j, f: (layer, i, f, 0))],
        out_specs=pl.BlockSpec((tm, d // 2), lambda i, j, f: (i * nt + j, 0)),
        out_shape=jax.ShapeDtypeStruct((e * cap, d // 2), jnp.uint32),
        scratch_shapes=[pltpu.VMEM((2, tm, d), F32), pltpu.VMEM((tm, d), BF16), pltpu.VMEM((tm, d), F32),
                        pltpu.SemaphoreType.DMA((2,))],
        compiler_params=_params(("arbitrary", "arbitrary", "arbitrary")),
        name="expert_ffn",
    )(idx3, idx3, h, gate2, wg, wu, wd)


WAIT_CHUNK = 256
COMBINE_COLS = 256
ISSUE_UNROLL = 8


def _combine_kernel(ent_ref, off_ref, jmax_ref, x_ref, ye_hbm, g_ref, o_ref, buf_ref, sem, *, tb, final_norm):
    t = pl.program_id(0)
    half = buf_ref.shape[2]
    slot = t % 2

    def fill(tile, s):
        for j in range(MAX_PLANES):
            @pl.when(j < jmax_ref[tile])
            def _():
                buf_ref[s, j * tb:(j + 1) * tb, :] = jnp.zeros((tb, half), jnp.uint32)
        lo = off_ref[tile]
        n_ent = off_ref[tile + 1] - lo

        def issue(p):
            ent = ent_ref[p]
            _row_copy(ye_hbm, ent & 0xFFFF, buf_ref.at[s], ent >> 16, sem.at[s]).start()

        def issue_group(q, carry):
            for r in range(ISSUE_UNROLL):
                issue(lo + q * ISSUE_UNROLL + r)
            return carry
        groups = n_ent // ISSUE_UNROLL
        lax.fori_loop(0, groups, issue_group, 0)

        def issue_rest(p, carry):
            issue(p)
            return carry
        lax.fori_loop(lo + groups * ISSUE_UNROLL, lo + n_ent, issue_rest, 0)

    @pl.when(t == 0)
    def _():
        fill(0, 0)

    @pl.when(t + 1 < pl.num_programs(0))
    def _():
        fill(t + 1, 1 - slot)

    def wait_rows(k):
        pltpu.make_async_copy(ye_hbm.at[pl.ds(0, k), :], buf_ref.at[slot, pl.ds(0, k), :], sem.at[slot]).wait()

    count = off_ref[t + 1] - off_ref[t]

    def bulk(p, carry):
        wait_rows(WAIT_CHUNK)
        return carry
    lax.fori_loop(0, count // WAIT_CHUNK, bulk, 0)
    k = WAIT_CHUNK // 2
    while k >= 1:
        @pl.when((count & k) != 0)
        def _():
            wait_rows(k)
        k //= 2

    jmax = jmax_ref[t]
    for c0 in range(0, half, COMBINE_COLS):
        cols = slice(c0, min(c0 + COMBINE_COLS, half))
        hi_cols = slice(half + cols.start, half + cols.stop)

        def add(j, acc):
            lo, hi = _unpack_bf16_pairs(buf_ref[slot, pl.ds(pl.multiple_of(j * tb, tb), tb), cols])
            return acc[0] + lo, acc[1] + hi
        o_ref[:, cols], o_ref[:, hi_cols] = lax.fori_loop(0, jmax, add, (x_ref[:, cols], x_ref[:, hi_cols]))

    if final_norm:
        o_ref[...] = _rms(o_ref[...], g_ref[...])


def _combine(x1, ye, entries, offsets, jmax, g, tb, final_norm):
    n, d = x1.shape
    assert ye.shape[0] >= WAIT_CHUNK and MAX_PLANES * tb >= WAIT_CHUNK
    assert ye.shape[1] * 2 == d and ye.dtype == jnp.uint32
    grid_spec = pltpu.PrefetchScalarGridSpec(
        num_scalar_prefetch=3,
        grid=(n // tb,),
        in_specs=[pl.BlockSpec((tb, d), lambda i, *_: (i, 0)),
                  pl.BlockSpec(memory_space=pl.ANY),
                  pl.BlockSpec((1, d), lambda i, *_: (0, 0))],
        out_specs=pl.BlockSpec((tb, d), lambda i, *_: (i, 0)),
        scratch_shapes=[pltpu.VMEM((2, MAX_PLANES * tb, d // 2), jnp.uint32), pltpu.SemaphoreType.DMA((2,))],
    )
    return pl.pallas_call(
        functools.partial(_combine_kernel, tb=tb, final_norm=final_norm),
        grid_spec=grid_spec,
        out_shape=jax.ShapeDtypeStruct((n, d), F32),
        compiler_params=_params(("arbitrary",)),
        name="combine",
    )(entries, offsets, jmax, x1, ye, g)


def _route(aff_t, tb):
    e, n = aff_t.shape
    cap = CAPACITY_FACTOR * n // e
    gate, idx = lax.top_k(aff_t, cap)
    total = e * cap
    pos = jnp.arange(total, dtype=jnp.int32)
    tok, src = lax.sort_key_val(idx.reshape(-1), pos)
    new_tok = jnp.concatenate([jnp.ones((1,), bool), tok[1:] != tok[:-1]])
    plane = pos - lax.cummax(jnp.where(new_tok, pos, 0))
    entries = src | ((plane * tb + tok % tb) << 16)
    tiles = jnp.arange(n // tb + 1, dtype=jnp.int32)
    offsets = jnp.sum(tok[None, :] < (tiles * tb)[:, None], axis=1, dtype=jnp.int32)
    in_tile = (tok // tb)[None, :] == tiles[:-1, None]
    jmax = jnp.max(jnp.where(in_tile, plane[None, :] + 1, 0), axis=1)
    return gate, idx, entries, offsets, jmax


def _rope_tables(t):
    pos = jnp.arange(t)
    row = (pos // GRID_W).astype(F32)
    col = (pos % GRID_W).astype(F32)
    inv = ROPE_THETA ** (-jnp.arange(0, AXIS_DIM, 2, dtype=F32) / AXIS_DIM)
    ar, ac = row[:, None] * inv[None, :], col[:, None] * inv[None, :]
    cos = jnp.concatenate([jnp.cos(ar), jnp.cos(ar), jnp.cos(ac), jnp.cos(ac)], axis=-1)
    sin = jnp.concatenate([-jnp.sin(ar), jnp.sin(ar), -jnp.sin(ac), jnp.sin(ac)], axis=-1)
    return cos, sin


def _trunk(x, p):
    b, t, d = x.shape
    n = b * t
    assert N_EXPERTS * (CAPACITY_FACTOR * n // N_EXPERTS) <= 1 << 16
    cos, sin = _rope_tables(t)
    tk = _tile(t, min(TILE_ATT_KV, max(t // ATT_MIN_KV_BLOCKS, M_CHUNK)))
    tq = _tile(t, TILE_ATT_Q * TILE_ATT_KV // tk)
    tb = _tile(n, TILE_COMBINE)
    xf = x.reshape(n, d)
    depth = p["w_in"].shape[0]
    for l in range(depth):
        proj, gates_t = _inproj(xf, p["norm1_g"][l], p["w_in"], l)
        proj = proj.reshape(b, t, IN_WIDTH_PAD)
        gates_t = jnp.swapaxes(gates_t.reshape(M_GATE_W, b, t), 0, 1)
        kr, vt = _kprep(proj, cos, sin, p["k_norm_g"][l], tk)
        att = _attention(proj, kr, vt, cos, sin, p["q_norm_g"][l], tq)
        qm, kt = _conv(proj, p["conv_w"][l], p["conv_b"][l])
        gate_bias = (p["b_in"][l], p["b_f"][l], p["b_col"][l])
        hfw = _mlstm(proj, qm, kt, gates_t, *gate_bias, reverse=False)
        mem = _mlstm(proj, qm, kt, gates_t, *gate_bias, reverse=True, hfw=hfw, gn=p["mlstm_norm_g"][l])
        x1, h2, aff_t = _outproj(xf, mem.reshape(n, M_V_W), att.reshape(n, ATT_Q_W), p["w_out"], l,
                                 p["norm2_g"][l], p["w_router_t"][l])
        gate, idx, entries, offsets, jmax = _route(aff_t, tb)
        ye = _ffn(h2, idx, gate, p["w_gate"], p["w_up"], p["w_down"], l)
        xf = _combine(x1, ye, entries, offsets, jmax, p["final_norm_g"], tb,
                      final_norm=(l == depth - 1))
    return xf.reshape(b, t, d)


def _prepare(norm1_g, w_in, conv_w, conv_b, b_gates, q_norm_g, k_norm_g, mlstm_norm_g, w_out,
             norm2_g, w_router, w_gate, w_up, w_down, final_norm_g):
    depth, d, _ = w_in.shape

    def forget_block(gates):
        pieces = []
        for kind in range(0, N_GATE_KINDS, 2):
            f_lo = (kind + 1) * M_HEADS
            pieces += [gates[..., f_lo:f_lo + M_HEADS], jnp.zeros(gates.shape[:-1] + (M_HEADS,), gates.dtype)]
        pieces.append(jnp.zeros(gates.shape[:-1] + (LANES - M_GATE_W,), gates.dtype))
        return jnp.concatenate(pieces, axis=-1)

    w_in_b = w_in.astype(BF16)
    first_block_pad = jnp.zeros((depth, d, COL_MG + LANES - IN_WIDTH), BF16)
    assert COL_MG + 2 * LANES == IN_WIDTH_PAD
    return {
        "norm1_g": norm1_g.reshape(depth, 1, d),
        "w_in": jnp.concatenate([w_in_b, first_block_pad, forget_block(w_in_b[..., COL_MG:])], axis=-1),
        "conv_w": conv_w,
        "conv_b": conv_b.reshape(depth, 1, -1),
        "b_in": jnp.pad(b_gates, ((0, 0), (0, LANES - M_GATE_W))).reshape(depth, 1, LANES),
        "b_f": forget_block(b_gates).reshape(depth, 1, LANES),
        "b_col": b_gates.reshape(depth, M_GATE_W, 1),
        "q_norm_g": q_norm_g.reshape(depth, 1, HEAD_DIM),
        "k_norm_g": k_norm_g.reshape(depth, 1, HEAD_DIM),
        "mlstm_norm_g": mlstm_norm_g.reshape(depth, 1, M_V_W),
        "w_out": w_out.astype(BF16),
        "norm2_g": norm2_g.reshape(depth, 1, d),
        "w_router_t": jnp.swapaxes(w_router, 1, 2),
        "w_gate": w_gate.astype(BF16),
        "w_up": w_up.astype(BF16),
        "w_down": w_down.astype(BF16),
        "final_norm_g": final_norm_g.reshape(1, d),
    }


def kernel(x_prompt, x_sample, norm1_g, w_in, conv_w, conv_b, b_gates, q_norm_g, k_norm_g, mlstm_norm_g,
           w_out, norm2_g, w_router, w_gate, w_up, w_down, final_norm_g):
    p = _prepare(norm1_g, w_in, conv_w, conv_b, b_gates, q_norm_g, k_norm_g, mlstm_norm_g, w_out,
                 norm2_g, w_router, w_gate, w_up, w_down, final_norm_g)
    return _trunk(x_prompt, p), _trunk(x_sample, p)
```

```python
import functools
import math

import jax
import jax.numpy as jnp
from jax import lax
from jax.experimental import pallas as pl
from jax.experimental.pallas import tpu as pltpu

F32 = jnp.float32
BF16 = jnp.bfloat16
EPS = 1e-6
LOG2E = 1.4426950408889634

LANES = 128
SUBLANES = 8
VMEM_BYTES_V7X = 64 * 1024 * 1024
VMEM_LIMIT = VMEM_BYTES_V7X - 8 * 1024 * 1024

GRID_W = 64
ATT_KV_HEADS = 2
ATT_GROUPS = 4
HEAD_DIM = 128
AXIS_DIM = HEAD_DIM // 2
ROPE_THETA = 10000.0
M_HEADS = 4
M_QK_DIM = 128
M_V_DIM = 256
M_CHUNK = 128
M_CONV = 5
N_GATE_KINDS = 4
N_EXPERTS = 16
CAPACITY_FACTOR = 2
ATT_Q_W = ATT_KV_HEADS * ATT_GROUPS * HEAD_DIM
ATT_KV_W = ATT_KV_HEADS * HEAD_DIM
M_QK_W = M_HEADS * M_QK_DIM
M_V_W = M_HEADS * M_V_DIM
M_GATE_W = N_GATE_KINDS * M_HEADS
COL_Q = 0
COL_K = COL_Q + ATT_Q_W
COL_V = COL_K + ATT_KV_W
COL_MQK = COL_V + ATT_KV_W
COL_MV = COL_MQK + 2 * M_QK_W
COL_MO = COL_MV + M_V_W
COL_MG = COL_MO + M_V_W
IN_WIDTH = COL_MG + M_GATE_W
IN_WIDTH_PAD = 4864
M_AUG = M_V_DIM + LANES
MAX_PLANES = N_EXPERTS

TILE_INPROJ = 256
TILE_ATT_Q = 256
TILE_ATT_KV = 1024
ATT_MIN_KV_BLOCKS = 4
TILE_CONV = 512
TILE_OUTPROJ = 512
TILE_FFN_ROWS = 512
TILE_FFN_HIDDEN = 512
TILE_COMBINE = 128
MLSTM_SEQS_PER_STEP = 4


def _tile(dim, pref):
    t = min(dim, pref)
    assert dim % t == 0, (dim, pref)
    return t


def _params(sem, vmem=VMEM_LIMIT):
    return pltpu.CompilerParams(dimension_semantics=sem, vmem_limit_bytes=vmem)


def _rms(x, g):
    return x * lax.rsqrt(jnp.mean(x * x, axis=-1, keepdims=True) + EPS) * g


def _log_sigmoid(x):
    return jnp.minimum(x, 0.0) - jnp.log1p(jnp.exp(-jnp.abs(x)))


def _rope(x, cos, sin):
    lane = lax.broadcasted_iota(jnp.int32, x.shape, 1)
    first = (lane % AXIS_DIM) < (AXIS_DIM // 2)
    swapped = jnp.where(first, pltpu.roll(x, LANES - AXIS_DIM // 2, axis=1),
                        pltpu.roll(x, AXIS_DIM // 2, axis=1))
    return x * cos + swapped * sin


def _inproj_kernel(x_ref, g_ref, w_ref, o_ref, gt_ref, *, col_chunk):
    h = _rms(x_ref[...], g_ref[...]).astype(BF16)
    width = o_ref.shape[1]
    for c0 in range(0, width, col_chunk):
        c1 = min(c0 + col_chunk, width)
        o_ref[:, c0:c1] = jnp.dot(h, w_ref[:, c0:c1], preferred_element_type=F32)
    gt_ref[...] = o_ref[:, COL_MG:COL_MG + LANES].T[:M_GATE_W, :]


def _inproj(x, g, w, layer):
    n, d = x.shape
    p = w.shape[2]
    tm = _tile(n, TILE_INPROJ)
    return pl.pallas_call(
        functools.partial(_inproj_kernel, col_chunk=512),
        grid=(n // tm,),
        in_specs=[pl.BlockSpec((tm, d), lambda i: (i, 0)),
                  pl.BlockSpec((1, d), lambda i: (0, 0)),
                  pl.BlockSpec((None, d, p), lambda i: (layer, 0, 0), pipeline_mode=pl.Buffered(1))],
        out_specs=[pl.BlockSpec((tm, p), lambda i: (i, 0)),
                   pl.BlockSpec((M_GATE_W, tm), lambda i: (0, i))],
        out_shape=[jax.ShapeDtypeStruct((n, p), F32),
                   jax.ShapeDtypeStruct((M_GATE_W, n), F32)],
        compiler_params=_params(("parallel",)),
        name="inproj",
    )(x, g, w)


def _kprep_kernel(k_ref, v_ref, cos_ref, sin_ref, g_ref, ko_ref, vt_ref):
    cos, sin, g = cos_ref[...], sin_ref[...], g_ref[...]
    for h in range(ATT_KV_HEADS):
        k = _rms(k_ref[:, h * HEAD_DIM:(h + 1) * HEAD_DIM], g)
        ko_ref[h] = _rope(k, cos, sin).astype(BF16)
        vt_ref[h, 0] = v_ref[:, h * HEAD_DIM:(h + 1) * HEAD_DIM].T.astype(BF16)


def _kprep(proj, cos, sin, gk, tk):
    b, t, _ = proj.shape
    nkb = t // tk
    return pl.pallas_call(
        _kprep_kernel,
        grid=(b, nkb),
        in_specs=[pl.BlockSpec((None, tk, ATT_KV_W), lambda i, j: (i, j, COL_K // ATT_KV_W)),
                  pl.BlockSpec((None, tk, ATT_KV_W), lambda i, j: (i, j, COL_V // ATT_KV_W)),
                  pl.BlockSpec((tk, HEAD_DIM), lambda i, j: (j, 0)),
                  pl.BlockSpec((tk, HEAD_DIM), lambda i, j: (j, 0)),
                  pl.BlockSpec((1, HEAD_DIM), lambda i, j: (0, 0))],
        out_specs=[pl.BlockSpec((None, ATT_KV_HEADS, tk, HEAD_DIM), lambda i, j: (i, 0, j, 0)),
                   pl.BlockSpec((None, ATT_KV_HEADS, 1, HEAD_DIM, tk), lambda i, j: (i, 0, j, 0, 0))],
        out_shape=[jax.ShapeDtypeStruct((b, ATT_KV_HEADS, t, HEAD_DIM), BF16),
                   jax.ShapeDtypeStruct((b, ATT_KV_HEADS, nkb, HEAD_DIM, tk), BF16)],
        compiler_params=_params(("parallel", "parallel")),
        name="kprep",
    )(proj, proj, cos, sin, gk)


def _attn_kernel(q_ref, cos_ref, sin_ref, g_ref, k_ref, vt_ref, o_ref, s_ref, *, tq, tk, nkb):
    cos, sin, g = cos_ref[...], sin_ref[...], g_ref[...]
    qscale = HEAD_DIM ** -0.5 * LOG2E
    qs = []
    for i in range(ATT_GROUPS):
        q = _rms(q_ref[:, i * HEAD_DIM:(i + 1) * HEAD_DIM], g)
        qs.append((_rope(q, cos, sin) * qscale).astype(BF16))
    q4 = jnp.concatenate(qs, axis=0)
    nq = ATT_GROUPS * tq

    def scores(j):
        kb = k_ref[pl.ds(pl.multiple_of(j * tk, tk), tk), :]
        return lax.dot_general(kb, q4, (((1,), (1,)), ((), ())), preferred_element_type=F32)

    def softmax_step(j, slot, carry):
        m, l, acc = carry
        s = s_ref[slot]
        m_new = jnp.maximum(m, jnp.max(s, axis=0, keepdims=True))
        alpha = jnp.exp2(m - m_new)
        p = jnp.exp2(s - m_new)
        l = alpha * l + jnp.sum(p, axis=0, keepdims=True)
        acc = alpha * acc + jnp.dot(vt_ref[j], p.astype(BF16), preferred_element_type=F32)
        return m_new, l, acc

    def pair(jj, carry):
        j = 2 * jj
        s_ref[1] = scores(j + 1)
        carry = softmax_step(j, 0, carry)
        s_ref[0] = scores(j + 2)
        return softmax_step(j + 1, 1, carry)

    s_ref[0] = scores(0)
    carry = (jnp.full((1, nq), -jnp.inf, F32), jnp.zeros((1, nq), F32), jnp.zeros((HEAD_DIM, nq), F32))
    if nkb > 1:
        assert nkb % 2 == 0
        carry = lax.fori_loop(0, nkb // 2 - 1, pair, carry)
        s_ref[1] = scores(nkb - 1)
        carry = softmax_step(nkb - 2, 0, carry)
        carry = softmax_step(nkb - 1, 1, carry)
    else:
        carry = softmax_step(0, 0, carry)
    _, l, acc = carry
    o = acc / l
    for i in range(ATT_GROUPS):
        o_ref[:, i * HEAD_DIM:(i + 1) * HEAD_DIM] = o[:, i * tq:(i + 1) * tq].T.astype(BF16)


def _attention(proj, kr, vt, cos, sin, gq, tq):
    b, t, _ = proj.shape
    nkb, tk = vt.shape[2], vt.shape[4]
    gw = ATT_GROUPS * HEAD_DIM
    return pl.pallas_call(
        functools.partial(_attn_kernel, tq=tq, tk=tk, nkb=nkb),
        grid=(b, ATT_KV_HEADS, t // tq),
        in_specs=[pl.BlockSpec((None, tq, gw), lambda i, h, j: (i, j, h)),
                  pl.BlockSpec((tq, HEAD_DIM), lambda i, h, j: (j, 0)),
                  pl.BlockSpec((tq, HEAD_DIM), lambda i, h, j: (j, 0)),
                  pl.BlockSpec((1, HEAD_DIM), lambda i, h, j: (0, 0)),
                  pl.BlockSpec((None, None, t, HEAD_DIM), lambda i, h, j: (i, h, 0, 0)),
                  pl.BlockSpec((None, None, nkb, HEAD_DIM, tk), lambda i, h, j: (i, h, 0, 0, 0))],
        out_specs=pl.BlockSpec((None, tq, gw), lambda i, h, j: (i, j, h)),
        out_shape=jax.ShapeDtypeStruct((b, t, ATT_Q_W), BF16),
        scratch_shapes=[pltpu.VMEM((2, tk, ATT_GROUPS * tq), F32)],
        compiler_params=_params(("parallel", "parallel", "arbitrary")),
        name="attention",
    )(proj, cos, sin, gq, kr, vt)


def _conv_kernel(xq_ref, xk_ref, pq_ref, pk_ref, nq_ref, nk_ref, w_ref, b_ref, q_ref, kt_ref, pad_ref, *, tc):
    r = pl.program_id(1)
    nr = pl.num_programs(1)
    halo = SUBLANES
    for c, (x_ref, prev_ref, next_ref) in enumerate(((xq_ref, pq_ref, nq_ref), (xk_ref, pk_ref, nk_ref))):
        cols = slice(c * M_QK_W, (c + 1) * M_QK_W)
        pad_ref[0:halo, cols] = jnp.where(r > 0, prev_ref[...], 0.0)
        pad_ref[halo:halo + tc, cols] = x_ref[...]
        pad_ref[halo + tc:2 * halo + tc, cols] = jnp.where(r < nr - 1, next_ref[...], 0.0)
    acc = jnp.broadcast_to(b_ref[...], (tc, 2 * M_QK_W))
    for k in range(M_CONV):
        lo = halo - M_CONV // 2 + k
        acc = acc + w_ref[k:k + 1, :] * pad_ref[lo:lo + tc, :]
    y = acc * jax.nn.sigmoid(acc)
    q_ref[...] = y[:, :M_QK_W].astype(BF16)
    for hd in range(M_HEADS):
        kh = y[:, M_QK_W + hd * M_QK_DIM:M_QK_W + (hd + 1) * M_QK_DIM] * (M_QK_DIM ** -0.5)
        kt_ref[hd] = kh.T.astype(BF16)


def _conv(proj, w, bias):
    b, t, _ = proj.shape
    cw = M_QK_W
    tc = _tile(t, TILE_CONV)
    hb = tc // SUBLANES
    c0 = COL_MQK // cw
    assert c0 * cw == COL_MQK
    last = t // SUBLANES - 1

    def prev_map(c):
        return lambda i, r: (i, jnp.maximum(r * hb - 1, 0), c)

    def next_map(c):
        return lambda i, r: (i, jnp.minimum((r + 1) * hb, last), c)

    return pl.pallas_call(
        functools.partial(_conv_kernel, tc=tc),
        grid=(b, t // tc),
        in_specs=[pl.BlockSpec((None, tc, cw), lambda i, r: (i, r, c0)),
                  pl.BlockSpec((None, tc, cw), lambda i, r: (i, r, c0 + 1)),
                  pl.BlockSpec((None, SUBLANES, cw), prev_map(c0)),
                  pl.BlockSpec((None, SUBLANES, cw), prev_map(c0 + 1)),
                  pl.BlockSpec((None, SUBLANES, cw), next_map(c0)),
                  pl.BlockSpec((None, SUBLANES, cw), next_map(c0 + 1)),
                  pl.BlockSpec((M_CONV, 2 * cw), lambda i, r: (0, 0)),
                  pl.BlockSpec((1, 2 * cw), lambda i, r: (0, 0))],
        out_specs=[pl.BlockSpec((None, tc, M_QK_W), lambda i, r: (i, r, 0)),
                   pl.BlockSpec((None, M_HEADS, M_QK_DIM, tc), lambda i, r: (i, 0, 0, r))],
        out_shape=[jax.ShapeDtypeStruct((b, t, M_QK_W), BF16),
                   jax.ShapeDtypeStruct((b, M_HEADS, M_QK_DIM, t), BF16)],
        scratch_shapes=[pltpu.VMEM((tc + 2 * SUBLANES, 2 * cw), F32)],
        compiler_params=_params(("parallel", "parallel")),
        name="conv",
    )(proj, proj, proj, proj, proj, proj, w, bias)


def _mlstm_kernel(*refs, reverse, bb):
    if reverse:
        (q_all, kt_all, vlo_ref, vhi_ref, gcol_ref, grow_ref, bi_ref, bf_ref, bcol_ref,
         hfw_all, olo_ref, ohi_ref, gn_ref, out_all, c_ref, m_ref) = refs
    else:
        (q_all, kt_all, vlo_ref, vhi_ref, gcol_ref, grow_ref, bi_ref, bf_ref, bcol_ref,
         out_all, c_ref, m_ref) = refs
    L = M_CHUNK

    @pl.when(pl.program_id(1) == 0)
    def _():
        c_ref[...] = jnp.zeros_like(c_ref)
        m_ref[...] = jnp.zeros_like(m_ref)

    ii = lax.broadcasted_iota(jnp.int32, (L, L), 0)
    jj = lax.broadcasted_iota(jnp.int32, (L, L), 1)
    mask = (jj >= ii) if reverse else (jj <= ii)
    tri = mask.astype(F32)
    tri_t = ((ii >= jj) if reverse else (ii <= jj)).astype(F32)
    kind = 2 * M_HEADS if reverse else 0
    lane = lax.broadcasted_iota(jnp.int32, (L, LANES), 1)
    row = lax.broadcasted_iota(jnp.int32, (L, LANES), 0)
    last_row = 0 if reverse else L - 1

    def running_max(a):
        s = 1
        while s < L:
            if reverse:
                shifted = jnp.where(row < L - s, pltpu.roll(a, L - s, axis=0), -jnp.inf)
            else:
                shifted = jnp.where(row >= s, pltpu.roll(a, s, axis=0), -jnp.inf)
            a = jnp.maximum(a, shifted)
            s *= 2
        return a

    m_all = m_ref[...]
    m_out = []
    for bi in range(bb):
        g_in = gcol_ref[bi, :, 0:LANES] + bi_ref[...]
        logf = _log_sigmoid(gcol_ref[bi, :, LANES:2 * LANES] + bf_ref[...])
        cum = jnp.dot(tri, logf, preferred_element_type=F32, precision=lax.Precision.HIGHEST)
        tot = jnp.sum(logf, axis=0, keepdims=True)
        m_prev = m_all[bi:bi + 1, :]
        u = jnp.maximum(running_max(g_in - cum), m_prev)
        u_last = u[last_row:last_row + 1, :]
        e_neg_m = jnp.exp(-(cum + u))
        s_prev_row = jnp.exp(m_prev - u_last)
        m_out.append(tot + u_last)
        gr = grow_ref[bi] + bcol_ref[...]
        cum_r = jnp.dot(_log_sigmoid(gr), tri_t, preferred_element_type=F32, precision=lax.Precision.HIGHEST)

        q_ref, out_ref = q_all.at[bi], out_all.at[bi]
        for hd in range(M_HEADS):
            st = bi * M_HEADS + hd
            li = kind + hd
            a_row = gr[li:li + 1, :] - cum_r[li + M_HEADS:li + M_HEADS + 1, :]
            u_b = jnp.broadcast_to(u[:, li:li + 1], (L, LANES))
            w_intra = jnp.where(mask, jnp.exp(a_row - u_b), 0.0)
            w_inter = jnp.exp(m_prev[:, li:li + 1] - u_b)
            q = q_ref[:, hd * M_QK_DIM:(hd + 1) * M_QK_DIM]
            kt = kt_all[bi, hd]
            v_ref = (vlo_ref if hd < M_HEADS // 2 else vhi_ref).at[bi]
            vo = (hd % (M_HEADS // 2)) * M_V_DIM
            ones_col = jnp.where(lane == li, 1.0, 0.0).astype(BF16)
            v_aug = jnp.concatenate([v_ref[:, vo:vo + M_V_DIM].astype(BF16), ones_col], axis=1)

            s = jnp.dot(q, kt, preferred_element_type=F32) * w_intra
            c_aug = c_ref[st]
            lhs = jnp.concatenate([s.astype(BF16), (q.astype(F32) * w_inter).astype(BF16)], axis=1)
            rhs = jnp.concatenate([v_aug, c_aug.astype(BF16)], axis=0)
            num = jnp.dot(lhs, rhs, preferred_element_type=F32)
            inv = 1.0 / jnp.maximum(jnp.abs(num[:, M_V_DIM:]), e_neg_m)
            h = num[:, :M_V_DIM] * inv[:, li:li + 1]

            w_g = jnp.exp(a_row - u_last[:, li:li + 1])
            kw = (kt.astype(F32) * w_g).astype(BF16)
            c_ref[st] = s_prev_row[:, li:li + 1] * c_aug + jnp.dot(kw, v_aug, preferred_element_type=F32)

            cs = slice(hd * M_V_DIM, (hd + 1) * M_V_DIM)
            if reverse:
                o_ref = (olo_ref if hd < M_HEADS // 2 else ohi_ref).at[bi]
                hs = hfw_all[bi, :, cs] + h
                y = _rms(hs, gn_ref[:, cs]) * jax.nn.sigmoid(o_ref[:, vo:vo + M_V_DIM])
                out_ref[:, cs] = y.astype(BF16)
            else:
                out_ref[:, cs] = h
    pad_rows = m_ref.shape[0] - bb
    m_ref[...] = jnp.concatenate(m_out + ([jnp.zeros((pad_rows, LANES), F32)] if pad_rows else []), axis=0)


def _mlstm(proj, qm, kt, gates_t, b_in, b_f, b_col, reverse, hfw=None, gn=None):
    b, t, _ = proj.shape
    nc = t // M_CHUNK
    half = M_V_W // 2
    bb = _tile(b, MLSTM_SEQS_PER_STEP)
    n_state = bb * M_HEADS
    assert bb <= SUBLANES and COL_MG % (2 * LANES) == 0

    def cidx(c):
        return nc - 1 - c if reverse else c

    in_specs = [pl.BlockSpec((bb, M_CHUNK, M_QK_W), lambda i, c: (i, cidx(c), 0)),
                pl.BlockSpec((bb, M_HEADS, M_QK_DIM, M_CHUNK), lambda i, c: (i, 0, 0, cidx(c))),
                pl.BlockSpec((bb, M_CHUNK, half), lambda i, c: (i, cidx(c), COL_MV // half)),
                pl.BlockSpec((bb, M_CHUNK, half), lambda i, c: (i, cidx(c), COL_MV // half + 1)),
                pl.BlockSpec((bb, M_CHUNK, 2 * LANES), lambda i, c: (i, cidx(c), COL_MG // (2 * LANES))),
                pl.BlockSpec((bb, M_GATE_W, M_CHUNK), lambda i, c: (i, 0, cidx(c))),
                pl.BlockSpec((1, LANES), lambda i, c: (0, 0)),
                pl.BlockSpec((1, LANES), lambda i, c: (0, 0)),
                pl.BlockSpec((M_GATE_W, 1), lambda i, c: (0, 0))]
    args = [qm, kt, proj, proj, proj, gates_t, b_in, b_f, b_col]
    if reverse:
        in_specs += [pl.BlockSpec((bb, M_CHUNK, M_V_W), lambda i, c: (i, cidx(c), 0)),
                     pl.BlockSpec((bb, M_CHUNK, half), lambda i, c: (i, cidx(c), COL_MO // half)),
                     pl.BlockSpec((bb, M_CHUNK, half), lambda i, c: (i, cidx(c), COL_MO // half + 1)),
                     pl.BlockSpec((1, M_V_W), lambda i, c: (0, 0))]
        args += [hfw, proj, proj, gn]
    return pl.pallas_call(
        functools.partial(_mlstm_kernel, reverse=reverse, bb=bb),
        grid=(b // bb, nc),
        in_specs=in_specs,
        out_specs=pl.BlockSpec((bb, M_CHUNK, M_V_W), lambda i, c: (i, cidx(c), 0)),
        out_shape=jax.ShapeDtypeStruct((b, t, M_V_W), BF16 if reverse else F32),
        scratch_shapes=[pltpu.VMEM((n_state, M_QK_DIM, M_AUG), F32),
                        pltpu.VMEM((SUBLANES, LANES), F32)],
        compiler_params=_params(("parallel", "arbitrary")),
        name="mlstm_bw" if reverse else "mlstm_fw",
    )(*args)


def _outproj_kernel(x_ref, mem_ref, att_ref, w_ref, g_ref, wr_ref, x1_ref, h_ref, aff_ref):
    half = mem_ref.shape[1]
    y = (jnp.dot(mem_ref[...], w_ref[0:half, :], preferred_element_type=F32)
         + jnp.dot(att_ref[...], w_ref[half:2 * half, :], preferred_element_type=F32))
    x1 = x_ref[...] + y
    x1_ref[...] = x1
    h = _rms(x1, g_ref[...])
    h_ref[...] = h
    h_hi = h.astype(BF16)
    h_lo = (h - h_hi.astype(F32)).astype(BF16)
    w = wr_ref[...]
    w_hi = w.astype(BF16)
    w_lo = (w - w_hi.astype(F32)).astype(BF16)
    nt_dims = (((1,), (1,)), ((), ()))
    logits = (lax.dot_general(w_hi, h_hi, nt_dims, preferred_element_type=F32)
              + lax.dot_general(w_lo, h_hi, nt_dims, preferred_element_type=F32)
              + lax.dot_general(w_hi, h_lo, nt_dims, preferred_element_type=F32))
    e = jnp.exp(logits - jnp.max(logits, axis=0, keepdims=True))
    aff_ref[...] = e / jnp.sum(e, axis=0, keepdims=True)


def _outproj(x, mem, att, w, layer, g, wr_t):
    n, d = x.shape
    half = mem.shape[1]
    tm = _tile(n, TILE_OUTPROJ)
    return pl.pallas_call(
        _outproj_kernel,
        grid=(n // tm,),
        in_specs=[pl.BlockSpec((tm, d), lambda i: (i, 0)),
                  pl.BlockSpec((tm, half), lambda i: (i, 0)),
                  pl.BlockSpec((tm, half), lambda i: (i, 0)),
                  pl.BlockSpec((None, 2 * half, d), lambda i: (layer, 0, 0), pipeline_mode=pl.Buffered(1)),
                  pl.BlockSpec((1, d), lambda i: (0, 0)),
                  pl.BlockSpec((N_EXPERTS, d), lambda i: (0, 0))],
        out_specs=[pl.BlockSpec((tm, d), lambda i: (i, 0)),
                   pl.BlockSpec((tm, d), lambda i: (i, 0)),
                   pl.BlockSpec((N_EXPERTS, tm), lambda i: (0, i))],
        out_shape=[jax.ShapeDtypeStruct((n, d), F32),
                   jax.ShapeDtypeStruct((n, d), F32),
                   jax.ShapeDtypeStruct((N_EXPERTS, n), F32)],
        compiler_params=_params(("parallel",)),
        name="outproj_router",
    )(x, mem, att, w, g, wr_t)


def _row_copy(src_hbm, src_row, dst, dst_row, sem):
    return pltpu.make_async_copy(src_hbm.at[pl.ds(src_row, 1), :], dst.at[pl.ds(dst_row, 1), :], sem)


def _ffn_kernel(idx_ref, idx_next_ref, h_hbm, gate_ref, wg_ref, wu_ref, wd_ref, o_ref, xf_ref, xb_ref, sem,
                *, tm):
    nt, nf = pl.num_programs(1), pl.num_programs(2)
    f = pl.program_id(2)
    tile = pl.program_id(0) * nt + pl.program_id(1)
    slot = tile % 2
    rows_per_step = tm // nf

    def tile_wait(s):
        pltpu.make_async_copy(h_hbm.at[pl.ds(0, tm), :], xf_ref.at[s], sem.at[s]).wait()

    @pl.when(f == 0)
    def _():
        @pl.when(tile == 0)
        def _():
            def issue(r, carry):
                _row_copy(h_hbm, idx_ref[0, 0, r], xf_ref.at[0], r, sem.at[0]).start()
                return carry
            lax.fori_loop(0, tm, issue, 0)
        tile_wait(slot)
        xb_ref[...] = xf_ref[slot].astype(BF16)
        o_ref[...] = jnp.zeros_like(o_ref)

    for r in range(rows_per_step):
        row = f * rows_per_step + r
        _row_copy(h_hbm, idx_next_ref[0, 0, row], xf_ref.at[1 - slot], row, sem.at[1 - slot]).start()

    x = xb_ref[...]
    a = jnp.dot(x, wg_ref[...], preferred_element_type=F32)
    u = jnp.dot(x, wu_ref[...], preferred_element_type=F32)
    hid = (a * jax.nn.sigmoid(a) * u).astype(BF16)
    o_ref[...] += jnp.dot(hid, wd_ref[...], preferred_element_type=F32) * gate_ref[...]

    @pl.when((tile == pl.num_programs(0) * nt - 1) & (f == nf - 1))
    def _():
        tile_wait(1 - slot)


def _ffn(h, idx, gate, wg, wu, wd, layer):
    n, d = h.shape
    e, cap = idx.shape
    ff = wg.shape[3]
    tm = _tile(cap, TILE_FFN_ROWS)
    tf = _tile(ff, TILE_FFN_HIDDEN)
    nt = cap // tm
    assert tm % (ff // tf) == 0
    idx3 = idx.reshape(e * nt, 1, tm)
    gate2 = gate.reshape(e * cap, 1)
    last = e * nt - 1
    return pl.pallas_call(
        functools.partial(_ffn_kernel, tm=tm),
        grid=(e, nt, ff // tf),
        in_specs=[pl.BlockSpec((1, 1, tm), lambda i, j, f: (i * nt + j, 0, 0), memory_space=pltpu.SMEM),
                  pl.BlockSpec((1, 1, tm), lambda i, j, f: (jnp.minimum(i * nt + j + 1, last), 0, 0),
                               memory_space=pltpu.SMEM),
                  pl.BlockSpec(memory_space=pl.ANY),
                  pl.BlockSpec((tm, 1), lambda i, j, f: (i * nt + j, 0)),
                  pl.BlockSpec((None, None, d, tf), lambda i, j, f: (layer, i, 0, f)),
                  pl.BlockSpec((None, None, d, tf), lambda i, j, f: (layer, i, 0, f)),
                  pl.BlockSpec((None, None, tf, d), lambda i, j, f: (layer, i, f, 0))],
        out_specs=pl.BlockSpec((tm, d), lambda i, j, f: (i * nt + j, 0)),
        out_shape=jax.ShapeDtypeStruct((e * cap, d), F32),
        scratch_shapes=[pltpu.VMEM((2, tm, d), F32), pltpu.VMEM((tm, d), BF16), pltpu.SemaphoreType.DMA((2,))],
        compiler_params=_params(("arbitrary", "arbitrary", "arbitrary")),
        name="expert_ffn",
    )(idx3, idx3, h, gate2, wg, wu, wd)


WAIT_CHUNK = 256
COMBINE_COLS = 256
ISSUE_UNROLL = 8


def _combine_kernel(ent_ref, off_ref, jmax_ref, x_ref, ye_hbm, g_ref, o_ref, buf_ref, sem, *, tb, final_norm):
    t = pl.program_id(0)
    d = buf_ref.shape[2]
    slot = t % 2

    def fill(tile, s):
        for j in range(MAX_PLANES):
            @pl.when(j < jmax_ref[tile])
            def _():
                buf_ref[s, j * tb:(j + 1) * tb, :] = jnp.zeros((tb, d), F32)
        lo = off_ref[tile]
        n_ent = off_ref[tile + 1] - lo

        def issue(p):
            ent = ent_ref[p]
            _row_copy(ye_hbm, ent & 0xFFFF, buf_ref.at[s], ent >> 16, sem.at[s]).start()

        def issue_group(q, carry):
            for r in range(ISSUE_UNROLL):
                issue(lo + q * ISSUE_UNROLL + r)
            return carry
        groups = n_ent // ISSUE_UNROLL
        lax.fori_loop(0, groups, issue_group, 0)

        def issue_rest(p, carry):
            issue(p)
            return carry
        lax.fori_loop(lo + groups * ISSUE_UNROLL, lo + n_ent, issue_rest, 0)

    @pl.when(t == 0)
    def _():
        fill(0, 0)

    @pl.when(t + 1 < pl.num_programs(0))
    def _():
        fill(t + 1, 1 - slot)

    def wait_rows(k):
        pltpu.make_async_copy(ye_hbm.at[pl.ds(0, k), :], buf_ref.at[slot, pl.ds(0, k), :], sem.at[slot]).wait()

    count = off_ref[t + 1] - off_ref[t]

    def bulk(p, carry):
        wait_rows(WAIT_CHUNK)
        return carry
    lax.fori_loop(0, count // WAIT_CHUNK, bulk, 0)
    k = WAIT_CHUNK // 2
    while k >= 1:
        @pl.when((count & k) != 0)
        def _():
            wait_rows(k)
        k //= 2

    jmax = jmax_ref[t]
    for c0 in range(0, d, COMBINE_COLS):
        cols = slice(c0, min(c0 + COMBINE_COLS, d))

        def add(j, acc):
            return acc + buf_ref[slot, pl.ds(pl.multiple_of(j * tb, tb), tb), cols]
        o_ref[:, cols] = lax.fori_loop(0, jmax, add, x_ref[:, cols])

    if final_norm:
        o_ref[...] = _rms(o_ref[...], g_ref[...])


def _combine(x1, ye, entries, offsets, jmax, g, tb, final_norm):
    n, d = x1.shape
    assert ye.shape[0] >= WAIT_CHUNK and MAX_PLANES * tb >= WAIT_CHUNK
    grid_spec = pltpu.PrefetchScalarGridSpec(
        num_scalar_prefetch=3,
        grid=(n // tb,),
        in_specs=[pl.BlockSpec((tb, d), lambda i, *_: (i, 0)),
                  pl.BlockSpec(memory_space=pl.ANY),
                  pl.BlockSpec((1, d), lambda i, *_: (0, 0))],
        out_specs=pl.BlockSpec((tb, d), lambda i, *_: (i, 0)),
        scratch_shapes=[pltpu.VMEM((2, MAX_PLANES * tb, d), F32), pltpu.SemaphoreType.DMA((2,))],
    )
    return pl.pallas_call(
        functools.partial(_combine_kernel, tb=tb, final_norm=final_norm),
        grid_spec=grid_spec,
        out_shape=jax.ShapeDtypeStruct((n, d), F32),
        compiler_params=_params(("arbitrary",)),
        name="combine",
    )(entries, offsets, jmax, x1, ye, g)


def _route(aff_t, tb):
    e, n = aff_t.shape
    cap = CAPACITY_FACTOR * n // e
    gate, idx = lax.top_k(aff_t, cap)
    total = e * cap
    pos = jnp.arange(total, dtype=jnp.int32)
    tok, src = lax.sort_key_val(idx.reshape(-1), pos)
    new_tok = jnp.concatenate([jnp.ones((1,), bool), tok[1:] != tok[:-1]])
    plane = pos - lax.cummax(jnp.where(new_tok, pos, 0))
    entries = src | ((plane * tb + tok % tb) << 16)
    tiles = jnp.arange(n // tb + 1, dtype=jnp.int32)
    offsets = jnp.sum(tok[None, :] < (tiles * tb)[:, None], axis=1, dtype=jnp.int32)
    in_tile = (tok // tb)[None, :] == tiles[:-1, None]
    jmax = jnp.max(jnp.where(in_tile, plane[None, :] + 1, 0), axis=1)
    return gate, idx, entries, offsets, jmax


def _rope_tables(t):
    pos = jnp.arange(t)
    row = (pos // GRID_W).astype(F32)
    col = (pos % GRID_W).astype(F32)
    inv = ROPE_THETA ** (-jnp.arange(0, AXIS_DIM, 2, dtype=F32) / AXIS_DIM)
    ar, ac = row[:, None] * inv[None, :], col[:, None] * inv[None, :]
    cos = jnp.concatenate([jnp.cos(ar), jnp.cos(ar), jnp.cos(ac), jnp.cos(ac)], axis=-1)
    sin = jnp.concatenate([-jnp.sin(ar), jnp.sin(ar), -jnp.sin(ac), jnp.sin(ac)], axis=-1)
    return cos, sin


def _trunk(x, p):
    b, t, d = x.shape
    n = b * t
    assert N_EXPERTS * (CAPACITY_FACTOR * n // N_EXPERTS) <= 1 << 16
    cos, sin = _rope_tables(t)
    tk = _tile(t, min(TILE_ATT_KV, max(t // ATT_MIN_KV_BLOCKS, M_CHUNK)))
    tq = _tile(t, TILE_ATT_Q * TILE_ATT_KV // tk)
    tb = _tile(n, TILE_COMBINE)
    xf = x.reshape(n, d)
    depth = p["w_in"].shape[0]
    for l in range(depth):
        proj, gates_t = _inproj(xf, p["norm1_g"][l], p["w_in"], l)
        proj = proj.reshape(b, t, IN_WIDTH_PAD)
        gates_t = jnp.swapaxes(gates_t.reshape(M_GATE_W, b, t), 0, 1)
        kr, vt = _kprep(proj, cos, sin, p["k_norm_g"][l], tk)
        att = _attention(proj, kr, vt, cos, sin, p["q_norm_g"][l], tq)
        qm, kt = _conv(proj, p["conv_w"][l], p["conv_b"][l])
        gate_bias = (p["b_in"][l], p["b_f"][l], p["b_col"][l])
        hfw = _mlstm(proj, qm, kt, gates_t, *gate_bias, reverse=False)
        mem = _mlstm(proj, qm, kt, gates_t, *gate_bias, reverse=True, hfw=hfw, gn=p["mlstm_norm_g"][l])
        x1, h2, aff_t = _outproj(xf, mem.reshape(n, M_V_W), att.reshape(n, ATT_Q_W), p["w_out"], l,
                                 p["norm2_g"][l], p["w_router_t"][l])
        gate, idx, entries, offsets, jmax = _route(aff_t, tb)
        ye = _ffn(h2, idx, gate, p["w_gate"], p["w_up"], p["w_down"], l)
        xf = _combine(x1, ye, entries, offsets, jmax, p["final_norm_g"], tb,
                      final_norm=(l == depth - 1))
    return xf.reshape(b, t, d)


def _prepare(norm1_g, w_in, conv_w, conv_b, b_gates, q_norm_g, k_norm_g, mlstm_norm_g, w_out,
             norm2_g, w_router, w_gate, w_up, w_down, final_norm_g):
    depth, d, _ = w_in.shape

    def forget_block(gates):
        pieces = []
        for kind in range(0, N_GATE_KINDS, 2):
            f_lo = (kind + 1) * M_HEADS
            pieces += [gates[..., f_lo:f_lo + M_HEADS], jnp.zeros(gates.shape[:-1] + (M_HEADS,), gates.dtype)]
        pieces.append(jnp.zeros(gates.shape[:-1] + (LANES - M_GATE_W,), gates.dtype))
        return jnp.concatenate(pieces, axis=-1)

    w_in_b = w_in.astype(BF16)
    first_block_pad = jnp.zeros((depth, d, COL_MG + LANES - IN_WIDTH), BF16)
    assert COL_MG + 2 * LANES == IN_WIDTH_PAD
    return {
        "norm1_g": norm1_g.reshape(depth, 1, d),
        "w_in": jnp.concatenate([w_in_b, first_block_pad, forget_block(w_in_b[..., COL_MG:])], axis=-1),
        "conv_w": conv_w,
        "conv_b": conv_b.reshape(depth, 1, -1),
        "b_in": jnp.pad(b_gates, ((0, 0), (0, LANES - M_GATE_W))).reshape(depth, 1, LANES),
        "b_f": forget_block(b_gates).reshape(depth, 1, LANES),
        "b_col": b_gates.reshape(depth, M_GATE_W, 1),
        "q_norm_g": q_norm_g.reshape(depth, 1, HEAD_DIM),
        "k_norm_g": k_norm_g.reshape(depth, 1, HEAD_DIM),
        "mlstm_norm_g": mlstm_norm_g.reshape(depth, 1, M_V_W),
        "w_out": w_out.astype(BF16),
        "norm2_g": norm2_g.reshape(depth, 1, d),
        "w_router_t": jnp.swapaxes(w_router, 1, 2),
        "w_gate": w_gate.astype(BF16),
        "w_up": w_up.astype(BF16),
        "w_down": w_down.astype(BF16),
        "final_norm_g": final_norm_g.reshape(1, d),
    }


def kernel(x_prompt, x_sample, norm1_g, w_in, conv_w, conv_b, b_gates, q_norm_g, k_norm_g, mlstm_norm_g,
           w_out, norm2_g, w_router, w_gate, w_up, w_down, final_norm_g):
    p = _prepare(norm1_g, w_in, conv_w, conv_b, b_gates, q_norm_g, k_norm_g, mlstm_norm_g, w_out,
                 norm2_g, w_router, w_gate, w_up, w_down, final_norm_g)
    return _trunk(x_prompt, p), _trunk(x_sample, p)
```
